```python
import jax, jax.numpy as jnp
from jax import lax
import numpy as np

D_MODEL = 1024
BATCH = 8
SEQ = 4096
DEPTH = 4

HEAD_DIM = 64
FOX_HEADS = 8
FOX_W = FOX_HEADS * HEAD_DIM
CONV_WIDTH = 512
CONV_K = 3
NSA_HEADS = 8
NSA_KV_GROUPS = 2
NSA_W = NSA_HEADS * HEAD_DIM
NSA_KV_W = NSA_KV_GROUPS * HEAD_DIM
CMP_LEN = 32
CMP_STRIDE = 16
SEL_LEN = 64
SEL_TOP = 16
WIN = 512
Q_BLOCK = 128
D_FF = 2816
FFN_CONV_K = 3
N_BRANCH = 3
EPS = 1e-6
NEG_INF = -1e30
FORCE_BONUS = 1e4

IN_SIZES = (D_MODEL, D_MODEL, D_MODEL,
            FOX_W, FOX_W, FOX_W, FOX_HEADS,
            CONV_WIDTH, CONV_WIDTH, CONV_WIDTH,
            NSA_W,
            NSA_KV_W, NSA_KV_W,
            NSA_KV_W, NSA_KV_W,
            NSA_KV_W, NSA_KV_W,
            NSA_HEADS * 3)
IN_DIM = sum(IN_SIZES)
FOX_F_OFFSET = 3 * D_MODEL + 3 * FOX_W

kernel_name = 'hybrid_fox_shortconv_nsa_block'


def rmsnorm(x, g):
    x32 = x.astype(jnp.float32)
    y = x32 * lax.rsqrt(jnp.mean(x32 * x32, axis=-1, keepdims=True) + EPS)
    return y.astype(x.dtype) * g


def causal_dwconv(u, w):
    k, c = w.shape
    return lax.conv_general_dilated(u, w[:, None, :].astype(u.dtype), window_strides=(1,),
                                    padding=[(k - 1, 0)], dimension_numbers=('NWC', 'WIO', 'NWC'),
                                    feature_group_count=c)


def masked_softmax(logits, valid):
    p = jax.nn.softmax(jnp.where(valid, logits, NEG_INF), axis=-1)
    return jnp.where(valid, p, 0.0)


def alibi_slopes(n):
    return np.array([2.0 ** (-8.0 * (i + 1) / n) for i in range(n)], dtype=np.float32)


def selection_map(n_sel, n_cmp):
    sel_start = np.arange(n_sel)[:, None] * SEL_LEN
    cmp_start = np.arange(n_cmp)[None, :] * CMP_STRIDE
    ov = np.minimum(sel_start + SEL_LEN, cmp_start + CMP_LEN) - np.maximum(sel_start, cmp_start)
    return (np.clip(ov, 0, None) / CMP_STRIDE).astype(np.float32)


def fox_attention(q, k, v, f_logit):
    b, s, _ = q.shape
    def heads(t):
        return t.reshape(b, s, FOX_HEADS, HEAD_DIM).transpose(0, 2, 1, 3)
    q, k, v = heads(q), heads(k), heads(v)
    cum = jnp.cumsum(jax.nn.log_sigmoid(f_logit.astype(jnp.float32)), axis=1).transpose(0, 2, 1)
    kpos = jnp.arange(s)
    scale = HEAD_DIM ** -0.5

    def block(i):
        q0 = i * Q_BLOCK
        qb = lax.dynamic_slice_in_dim(q, q0, Q_BLOCK, axis=2)
        cq = lax.dynamic_slice_in_dim(cum, q0, Q_BLOCK, axis=2)
        qpos = q0 + jnp.arange(Q_BLOCK)
        logits = jnp.einsum('bhqd,bhkd->bhqk', qb, k, preferred_element_type=jnp.float32) * scale
        logits = logits + cq[..., :, None] - cum[..., None, :]
        p = masked_softmax(logits, kpos[None, :] <= qpos[:, None])
        return jnp.einsum('bhqk,bhkd->bhqd', p.astype(v.dtype), v)

    out = lax.map(block, jnp.arange(s // Q_BLOCK))
    return out.transpose(1, 0, 3, 2, 4).reshape(b, s, FOX_W)


def nsa_attention(q, kc, vc, ks, vs, kw, vw, gate_logit, pe_k, w1_k, w2_k, pe_v, w1_v, w2_v):
    b, s, _ = q.shape
    G, HG, dh = NSA_KV_GROUPS, NSA_HEADS // NSA_KV_GROUPS, HEAD_DIM
    q = q.reshape(b, s, G, HG, dh).transpose(0, 2, 3, 1, 4)
    gates = jax.nn.sigmoid(gate_logit.astype(jnp.float32)).reshape(b, s, G, HG, 3).transpose(0, 2, 3, 1, 4)

    def groups(t):
        return t.reshape(b, s, G, dh).transpose(0, 2, 1, 3)

    n_cmp = (s - CMP_LEN) // CMP_STRIDE + 1
    cmp_idx = np.arange(n_cmp)[:, None] * CMP_STRIDE + np.arange(CMP_LEN)[None, :]
    cmp_end = jnp.asarray(cmp_idx[:, -1])

    def compress(t, pe, w1, w2):
        blk = groups(t)[:, :, cmp_idx]
        blk = (blk + pe).reshape(b, G, n_cmp, CMP_LEN * dh)
        return jax.nn.gelu(blk @ w1) @ w2

    k_cmp = compress(kc, pe_k, w1_k, w2_k)
    v_cmp = compress(vc, pe_v, w1_v, w2_v)
    n_sel = s // SEL_LEN
    n_top = min(SEL_TOP, n_sel)
    k_sel = groups(ks).reshape(b, G, n_sel, SEL_LEN, dh)
    v_sel = groups(vs).reshape(b, G, n_sel, SEL_LEN, dh)
    sel_map = jnp.asarray(selection_map(n_sel, n_cmp))
    pad = ((0, 0), (0, 0), (WIN, 0), (0, 0))
    k_win = jnp.pad(groups(kw), pad)
    v_win = jnp.pad(groups(vw), pad)

    slopes = jnp.asarray(alibi_slopes(NSA_HEADS)).reshape(1, G, HG, 1, 1)
    scale = dh ** -0.5
    bi = jnp.arange(b)[:, None, None, None]
    gi = jnp.arange(G)[None, :, None, None]
    sel_blk = jnp.arange(n_sel)
    n_keys_sel = n_top * SEL_LEN

    def block(i):
        q0 = i * Q_BLOCK
        qb = lax.dynamic_slice_in_dim(q, q0, Q_BLOCK, axis=3)
        gb = lax.dynamic_slice_in_dim(gates, q0, Q_BLOCK, axis=3)
        qpos = q0 + jnp.arange(Q_BLOCK)
        dist_c = (qpos[:, None] - cmp_end[None, :]).astype(jnp.float32)
        lc = jnp.einsum('bghqd,bgcd->bghqc', qb, k_cmp, preferred_element_type=jnp.float32) * scale - slopes * dist_c
        pc = masked_softmax(lc, dist_c >= 0)
        o_c = jnp.einsum('bghqc,bgcd->bghqd', pc.astype(v_cmp.dtype), v_cmp)
        imp = jnp.einsum('bghqc,jc->bgqj', pc, sel_map)
        q_blk = qpos // SEL_LEN
        valid_blk = sel_blk[None, :] * SEL_LEN <= qpos[:, None]
        forced = (sel_blk[None, :] == 0) | (sel_blk[None, :] == q_blk[:, None]) | (sel_blk[None, :] == q_blk[:, None] - 1)
        score = jnp.where(valid_blk, imp + jnp.where(forced, FORCE_BONUS, 0.0), NEG_INF)
        _, top = lax.top_k(score, n_top)
        kg = k_sel[bi, gi, top].reshape(b, G, Q_BLOCK, n_keys_sel, dh)
        vg = v_sel[bi, gi, top].reshape(b, G, Q_BLOCK, n_keys_sel, dh)
        kpos_s = (top[..., None] * SEL_LEN + jnp.arange(SEL_LEN)).reshape(b, G, 1, Q_BLOCK, n_keys_sel)
        dist_s = (qpos[:, None] - kpos_s).astype(jnp.float32)
        ls = jnp.einsum('bghqd,bgqkd->bghqk', qb, kg, preferred_element_type=jnp.float32) * scale - slopes * dist_s
        ps = masked_softmax(ls, dist_s >= 0)
        o_s = jnp.einsum('bghqk,bgqkd->bghqd', ps.astype(vg.dtype), vg)
        kb = lax.dynamic_slice_in_dim(k_win, q0, WIN + Q_BLOCK, axis=2)
        vb = lax.dynamic_slice_in_dim(v_win, q0, WIN + Q_BLOCK, axis=2)
        kpos_w = q0 - WIN + jnp.arange(WIN + Q_BLOCK)
        dist_w = qpos[:, None] - kpos_w[None, :]
        valid_w = (dist_w >= 0) & (dist_w < WIN) & (kpos_w[None, :] >= 0)
        lw = jnp.einsum('bghqd,bgkd->bghqk', qb, kb, preferred_element_type=jnp.float32) * scale - slopes * dist_w.astype(jnp.float32)
        pw = masked_softmax(lw, valid_w)
        o_w = jnp.einsum('bghqk,bgkd->bghqd', pw.astype(vb.dtype), vb)
        out = gb[..., 0:1] * o_c + gb[..., 1:2] * o_s + gb[..., 2:3] * o_w
        return out.astype(q.dtype)

    out = lax.map(block, jnp.arange(s // Q_BLOCK))
    return out.transpose(1, 0, 4, 2, 3, 5).reshape(b, s, NSA_W)


def setup_inputs(seed: int = 0) -> dict:
    key = jax.random.key(seed)
    ks = jax.random.split(key, 24)
    f32 = jnp.float32

    def nrm(k, shape, scale):
        return jax.random.normal(k, shape, f32) * scale

    res_scale = (2 * DEPTH) ** -0.5
    x = nrm(ks[0], (BATCH, SEQ, D_MODEL), 1.0)
    norm_mix_g = 1.0 + nrm(ks[1], (DEPTH, D_MODEL), 0.02)
    w_in = nrm(ks[2], (DEPTH, D_MODEL, IN_DIM), D_MODEL ** -0.5)
    b_in = nrm(ks[3], (DEPTH, IN_DIM), 0.02)
    f_bias = jax.random.uniform(ks[4], (DEPTH, FOX_HEADS), f32, 2.0, 5.0)
    b_in = b_in.at[:, FOX_F_OFFSET:FOX_F_OFFSET + FOX_HEADS].set(f_bias)
    cmp_pe_k = nrm(ks[5], (DEPTH, CMP_LEN, HEAD_DIM), 0.1)
    cmp_w1_k = nrm(ks[6], (DEPTH, CMP_LEN * HEAD_DIM, HEAD_DIM), (CMP_LEN * HEAD_DIM) ** -0.5)
    cmp_w2_k = nrm(ks[7], (DEPTH, HEAD_DIM, HEAD_DIM), HEAD_DIM ** -0.5)
    cmp_pe_v = nrm(ks[8], (DEPTH, CMP_LEN, HEAD_DIM), 0.1)
    cmp_w1_v = nrm(ks[9], (DEPTH, CMP_LEN * HEAD_DIM, HEAD_DIM), (CMP_LEN * HEAD_DIM) ** -0.5)
    cmp_w2_v = nrm(ks[10], (DEPTH, HEAD_DIM, HEAD_DIM), HEAD_DIM ** -0.5)
    sc_conv_w = nrm(ks[11], (DEPTH, CONV_K, CONV_WIDTH), CONV_K ** -0.5)
    w_br_a = nrm(ks[12], (DEPTH, FOX_W, D_MODEL), FOX_W ** -0.5)
    w_br_b = nrm(ks[13], (DEPTH, CONV_WIDTH, D_MODEL), CONV_WIDTH ** -0.5)
    w_br_c = nrm(ks[14], (DEPTH, NSA_W, D_MODEL), NSA_W ** -0.5)
    w_o = nrm(ks[15], (DEPTH, D_MODEL, D_MODEL), D_MODEL ** -0.5 * res_scale)
    norm_ffn_g = 1.0 + nrm(ks[16], (DEPTH, D_MODEL), 0.02)
    w_up = nrm(ks[17], (DEPTH, D_MODEL, 2 * D_FF), D_MODEL ** -0.5)
    ffn_conv_w = nrm(ks[18], (DEPTH, FFN_CONV_K, D_FF), FFN_CONV_K ** -0.5)
    w_down = nrm(ks[19], (DEPTH, D_FF, D_MODEL), D_FF ** -0.5 * res_scale)
    norm_final_g = 1.0 + nrm(ks[20], (D_MODEL,), 0.02)
    return {'x': x, 'norm_mix_g': norm_mix_g, 'w_in': w_in, 'b_in': b_in,
            'cmp_pe_k': cmp_pe_k, 'cmp_w1_k': cmp_w1_k, 'cmp_w2_k': cmp_w2_k,
            'cmp_pe_v': cmp_pe_v, 'cmp_w1_v': cmp_w1_v, 'cmp_w2_v': cmp_w2_v,
            'sc_conv_w': sc_conv_w, 'w_br_a': w_br_a, 'w_br_b': w_br_b, 'w_br_c': w_br_c,
            'w_o': w_o, 'norm_ffn_g': norm_ffn_g, 'w_up': w_up, 'ffn_conv_w': ffn_conv_w,
            'w_down': w_down, 'norm_final_g': norm_final_g}


def reference(x, norm_mix_g, w_in, b_in, cmp_pe_k, cmp_w1_k, cmp_w2_k, cmp_pe_v, cmp_w1_v, cmp_w2_v,
              sc_conv_w, w_br_a, w_br_b, w_br_c, w_o, norm_ffn_g, w_up, ffn_conv_w, w_down, norm_final_g):
    splits = np.cumsum(IN_SIZES)[:-1].tolist()
    for l in range(DEPTH):
        h = rmsnorm(x, norm_mix_g[l])
        z = h @ w_in[l] + b_in[l]
        (ga, gb, gc, fq, fk, fv, ff, cb, cc, cx,
         nq, nkc, nvc, nks, nvs, nkw, nvw, ng) = jnp.split(z, splits, axis=-1)
        y_a = fox_attention(fq, fk, fv, ff) @ w_br_a[l]
        y_b = (cb * causal_dwconv(cc * cx, sc_conv_w[l])) @ w_br_b[l]
        y_c = nsa_attention(nq, nkc, nvc, nks, nvs, nkw, nvw, ng,
                            cmp_pe_k[l], cmp_w1_k[l], cmp_w2_k[l],
                            cmp_pe_v[l], cmp_w1_v[l], cmp_w2_v[l]) @ w_br_c[l]
        mix = jax.nn.sigmoid(ga) * y_a + jax.nn.sigmoid(gb) * y_b + jax.nn.sigmoid(gc) * y_c
        x = x + mix @ w_o[l]
        h2 = rmsnorm(x, norm_ffn_g[l])
        u, v = jnp.split(h2 @ w_up[l], 2, axis=-1)
        u = causal_dwconv(u, ffn_conv_w[l])
        x = x + (jax.nn.silu(u) * v) @ w_down[l]
    return rmsnorm(x, norm_final_g)
```

```python
import functools

import numpy as np
import jax
import jax.numpy as jnp
from jax import lax
from jax.experimental import pallas as pl
from jax.experimental.pallas import tpu as pltpu

F32 = jnp.float32
BF16 = jnp.bfloat16

HEAD_DIM = 64
FOX_HEADS = 8
NSA_HEADS = 8
NSA_KV_GROUPS = 2
NSA_HG = NSA_HEADS // NSA_KV_GROUPS
CMP_LEN = 32
CMP_STRIDE = 16
SEL_LEN = 64
SEL_TOP = 16
WIN = 512
Q_BLOCK = 128
EPS = 1e-6
NEG_INF = -1e30
PICKED = -3e38
FORCE_BONUS = 1e4
SCALE = HEAD_DIM ** -0.5

LANES = 128
SUBLANES = 8
VMEM_LIMIT = 48 * 1024 * 1024

NT_DIMS = (((1,), (1,)), ((), ()))


def _cparams(n_parallel, n_arbitrary=0):
    return pltpu.CompilerParams(
        dimension_semantics=("parallel",) * n_parallel + ("arbitrary",) * n_arbitrary,
        vmem_limit_bytes=VMEM_LIMIT)


def _sigmoid(x):
    return 1.0 / (1.0 + jnp.exp(-x))


def _rms(x, g):
    ms = jnp.mean(x * x, axis=-1, keepdims=True)
    return x * lax.rsqrt(ms + EPS) * g


def _norm_kernel(x_ref, g_ref, o_ref):
    o_ref[...] = _rms(x_ref[...], g_ref[...]).astype(o_ref.dtype)


def rmsnorm_rows(x, g, out_dtype, tm=1024):
    n, d = x.shape
    tm = min(tm, n)
    return pl.pallas_call(
        _norm_kernel,
        out_shape=jax.ShapeDtypeStruct((n, d), out_dtype),
        grid=(n // tm,),
        in_specs=[pl.BlockSpec((tm, d), lambda i: (i, 0)),
                  pl.BlockSpec((1, d), lambda i: (0, 0))],
        out_specs=pl.BlockSpec((tm, d), lambda i: (i, 0)),
        compiler_params=_cparams(1),
        name="rmsnorm",
    )(x, g.reshape(1, d))


def _mm_bias_kernel(a_ref, w_ref, b_ref, o_ref):
    acc = jnp.dot(a_ref[...], w_ref[...], preferred_element_type=F32)
    o_ref[...] = (acc + b_ref[...]).astype(o_ref.dtype)


def matmul_bias(a, w, b, out_dtype, tn, tm=1024):
    n, k = a.shape
    m = w.shape[1]
    tm = min(tm, n)
    return pl.pallas_call(
        _mm_bias_kernel,
        out_shape=jax.ShapeDtypeStruct((n, m), out_dtype),
        grid=(n // tm, m // tn),
        in_specs=[pl.BlockSpec((tm, k), lambda i, j: (i, 0)),
                  pl.BlockSpec((k, tn), lambda i, j: (0, j)),
                  pl.BlockSpec((1, tn), lambda i, j: (0, j))],
        out_specs=pl.BlockSpec((tm, tn), lambda i, j: (i, j)),
        compiler_params=_cparams(2),
        name="matmul_bias",
    )(a, w, b.reshape(1, m))


def _res_norm_kernel(a_ref, w_ref, x_ref, g_ref, xo_ref, ho_ref):
    xn = x_ref[...] + jnp.dot(a_ref[...], w_ref[...], preferred_element_type=F32)
    xo_ref[...] = xn
    ho_ref[...] = _rms(xn, g_ref[...]).astype(ho_ref.dtype)


def _res_norm_only_kernel(a_ref, w_ref, x_ref, g_ref, ho_ref):
    xn = x_ref[...] + jnp.dot(a_ref[...], w_ref[...], preferred_element_type=F32)
    ho_ref[...] = _rms(xn, g_ref[...]).astype(ho_ref.dtype)


def residual_matmul_norm(a, w, x, g, norm_dtype, keep_residual=True, tm=512):
    n, k = a.shape
    d = w.shape[1]
    tm = min(tm, n)
    row = lambda i: (i, 0)
    in_specs = [pl.BlockSpec((tm, k), row),
                pl.BlockSpec((k, d), lambda i: (0, 0)),
                pl.BlockSpec((tm, d), row),
                pl.BlockSpec((1, d), lambda i: (0, 0))]
    if keep_residual:
        return pl.pallas_call(
            _res_norm_kernel,
            out_shape=(jax.ShapeDtypeStruct((n, d), F32), jax.ShapeDtypeStruct((n, d), norm_dtype)),
            grid=(n // tm,), in_specs=in_specs,
            out_specs=(pl.BlockSpec((tm, d), row), pl.BlockSpec((tm, d), row)),
            compiler_params=_cparams(1), name="residual_matmul_norm",
        )(a, w, x, g.reshape(1, d))
    return pl.pallas_call(
        _res_norm_only_kernel,
        out_shape=jax.ShapeDtypeStruct((n, d), norm_dtype),
        grid=(n // tm,), in_specs=in_specs,
        out_specs=pl.BlockSpec((tm, d), row),
        compiler_params=_cparams(1), name="residual_matmul_final_norm",
    )(a, w, x, g.reshape(1, d))


CUM_BLOCK = 256


def _logsig_cumsum_kernel(f_ref, o_ref):
    h, s = f_ref.shape
    x = f_ref[...]
    ls = jnp.minimum(x, 0.0) - jnp.log1p(jnp.exp(-jnp.abs(x)))
    r = lax.broadcasted_iota(jnp.int32, (CUM_BLOCK, CUM_BLOCK), 0)
    c = lax.broadcasted_iota(jnp.int32, (CUM_BLOCK, CUM_BLOCK), 1)
    tri = jnp.where(r <= c, 1.0, 0.0).astype(BF16)
    carry = jnp.zeros((h, 1), F32)
    for blk in range(s // CUM_BLOCK):
        seg = ls[:, blk * CUM_BLOCK:(blk + 1) * CUM_BLOCK]
        hi = seg.astype(BF16)
        r1 = seg - hi.astype(F32)
        mid = r1.astype(BF16)
        lo = (r1 - mid.astype(F32)).astype(BF16)
        cs = (jnp.dot(hi, tri, preferred_element_type=F32)
              + jnp.dot(mid, tri, preferred_element_type=F32)
              + jnp.dot(lo, tri, preferred_element_type=F32)) + carry
        o_ref[:, blk * CUM_BLOCK:(blk + 1) * CUM_BLOCK] = cs
        carry = cs[:, CUM_BLOCK - 1:CUM_BLOCK]


def logsig_cumsum(f_t):
    b, h, s = f_t.shape
    return pl.pallas_call(
        _logsig_cumsum_kernel,
        out_shape=jax.ShapeDtypeStruct((b, h, s), F32),
        grid=(b,),
        in_specs=[pl.BlockSpec((None, h, s), lambda i: (i, 0, 0))],
        out_specs=pl.BlockSpec((None, h, s), lambda i: (i, 0, 0)),
        compiler_params=_cparams(1), name="logsig_cumsum",
    )(f_t)


def _fox_kernel(q_ref, kv_ref, cq_ref, ck_ref, o_ref, m_sc, l_sc, acc_sc, *, tile):
    i = pl.program_id(2)
    j = pl.program_id(3)

    @pl.when(j == 0)
    def _():
        m_sc[...] = jnp.full_like(m_sc, NEG_INF)
        l_sc[...] = jnp.zeros_like(l_sc)
        acc_sc[...] = jnp.zeros_like(acc_sc)

    @pl.when(j <= i)
    def _():
        kv = kv_ref[...]
        s = lax.dot_general(q_ref[...], kv, NT_DIMS, preferred_element_type=F32)
        s = s * SCALE + cq_ref[...] - ck_ref[...]
        qpos = i * tile + lax.broadcasted_iota(jnp.int32, (tile, tile), 0)
        kpos = j * tile + lax.broadcasted_iota(jnp.int32, (tile, tile), 1)
        s = jnp.where(kpos <= qpos, s, NEG_INF)
        m_prev = m_sc[...]
        m_new = jnp.maximum(m_prev, jnp.max(s, axis=-1, keepdims=True))
        alpha = jnp.exp(m_prev - m_new)
        p = jnp.exp(s - m_new)
        l_sc[...] = alpha * l_sc[...] + jnp.sum(p, axis=-1, keepdims=True)
        acc_sc[...] = alpha * acc_sc[...] + jnp.dot(p.astype(BF16), kv, preferred_element_type=F32)
        m_sc[...] = m_new

    @pl.when(j == i)
    def _():
        o_ref[...] = (acc_sc[...] / l_sc[...])[:, :HEAD_DIM].astype(o_ref.dtype)


def fox_attention(qp, kv, cum_col, cum_row, tile=512):
    b, h, s, _ = qp.shape
    tile = min(tile, s)
    nb = s // tile
    return pl.pallas_call(
        functools.partial(_fox_kernel, tile=tile),
        out_shape=jax.ShapeDtypeStruct((b, h, s, HEAD_DIM), BF16),
        grid=(b, h, nb, nb),
        in_specs=[pl.BlockSpec((None, None, tile, LANES), lambda b_, h_, i, j: (b_, h_, i, 0)),
                  pl.BlockSpec((None, None, tile, LANES), lambda b_, h_, i, j: (b_, h_, jnp.minimum(j, i), 0)),
                  pl.BlockSpec((None, None, tile, 1), lambda b_, h_, i, j: (b_, h_, i, 0)),
                  pl.BlockSpec((None, None, 1, tile), lambda b_, h_, i, j: (b_, h_, 0, jnp.minimum(j, i)))],
        out_specs=pl.BlockSpec((None, None, tile, HEAD_DIM), lambda b_, h_, i, j: (b_, h_, i, 0)),
        scratch_shapes=[pltpu.VMEM((tile, 1), F32), pltpu.VMEM((tile, 1), F32),
                        pltpu.VMEM((tile, LANES), F32)],
        compiler_params=_cparams(3, 1), name="fox_attention",
    )(qp, kv, cum_col, cum_row)


def _gelu_tanh(x):
    return 0.5 * x * (1.0 + jnp.tanh(np.sqrt(2.0 / np.pi).astype(np.float32) * (x + 0.044715 * (x * x * x))))


def _compress_kernel(bk_ref, bv_ref, pek_ref, pev_ref, w1k_ref, w2k_ref, w1v_ref, w2v_ref, o_ref):
    def mlp(blk_ref, pe_ref, w1_ref, w2_ref):
        xin = (blk_ref[...] + pe_ref[...]).astype(BF16)
        hid = _gelu_tanh(jnp.dot(xin, w1_ref[...], preferred_element_type=F32))
        return jnp.dot(hid.astype(BF16), w2_ref[...], preferred_element_type=F32)

    k_cmp = mlp(bk_ref, pek_ref, w1k_ref, w2k_ref)
    v_cmp = mlp(bv_ref, pev_ref, w1v_ref, w2v_ref)
    o_ref[...] = jnp.concatenate([v_cmp, k_cmp], axis=-1).astype(o_ref.dtype)


def compress_blocks(blk_k, blk_v, pe_k, pe_v, w1_k, w2_k, w1_v, w2_v):
    b, g, nc, feat = blk_k.shape
    blk_spec = pl.BlockSpec((None, None, nc, feat), lambda i, j: (i, j, 0, 0))
    full = lambda shape: pl.BlockSpec(shape, lambda i, j: (0,) * len(shape))
    return pl.pallas_call(
        _compress_kernel,
        out_shape=jax.ShapeDtypeStruct((b, g, nc, LANES), BF16),
        grid=(b, g),
        in_specs=[blk_spec, blk_spec, full((1, feat)), full((1, feat)),
                  full((feat, HEAD_DIM)), full((HEAD_DIM, HEAD_DIM)),
                  full((feat, HEAD_DIM)), full((HEAD_DIM, HEAD_DIM))],
        out_specs=pl.BlockSpec((None, None, nc, LANES), lambda i, j: (i, j, 0, 0)),
        compiler_params=_cparams(2), name="nsa_compress",
    )(blk_k, blk_v, pe_k, pe_v, w1_k, w2_k, w1_v, w2_v)


NSA_ROWS = NSA_HG * Q_BLOCK
SEL_TILE = 512


def _alibi_slope_column(group):
    row = lax.broadcasted_iota(jnp.int32, (NSA_ROWS, 1), 0)
    hg = row >> 7
    slope = jnp.zeros((NSA_ROWS, 1), F32)
    for gg in range(NSA_KV_GROUPS):
        for hh in range(NSA_HG):
            val = 2.0 ** (-8.0 * (gg * NSA_HG + hh + 1) / NSA_HEADS)
            slope = jnp.where((group == gg) & (hg == hh), np.float32(val), slope)
    return slope


def _nsa_kernel(q_ref, kvc_ref, kvs_ref, kvw_ref, gl_ref, selmap_ref, o_ref,
                m_sc, l_sc, acc_sc, *, seq, n_top):
    group = pl.program_id(1)
    i = pl.program_id(2)
    q0 = i * Q_BLOCK
    n_cmp_pad = seq // CMP_STRIDE
    n_sel = seq // SEL_LEN
    q = q_ref[...]
    slope = _alibi_slope_column(group)
    slope3 = slope.reshape(NSA_HG, Q_BLOCK, 1)
    row = lax.broadcasted_iota(jnp.int32, (NSA_ROWS, 1), 0)
    qpos_col = q0 + (row & (Q_BLOCK - 1))

    kvc = kvc_ref[...]
    cend = lax.broadcasted_iota(jnp.int32, (1, n_cmp_pad), 1) * CMP_STRIDE + (CMP_LEN - 1)
    dist_c = (qpos_col - cend).astype(F32)
    lc = lax.dot_general(q, kvc, NT_DIMS, preferred_element_type=F32) * SCALE - slope * dist_c
    valid_c = dist_c >= 0
    lc = jnp.where(valid_c, lc, NEG_INF)
    mc = jnp.max(lc, axis=-1, keepdims=True)
    pc = jnp.where(valid_c, jnp.exp(lc - mc), 0.0)
    sc = jnp.sum(pc, axis=-1, keepdims=True)
    pc = pc / jnp.where(sc > 0, sc, 1.0)
    o_c = jnp.dot(pc.astype(BF16), kvc, preferred_element_type=F32)

    pcs = pc[0:Q_BLOCK]
    for hh in range(1, NSA_HG):
        pcs = pcs + pc[hh * Q_BLOCK:(hh + 1) * Q_BLOCK]
    hi = pcs.astype(BF16)
    lo = (pcs - hi.astype(F32)).astype(BF16)
    selmap = selmap_ref[...]
    imp = (jnp.dot(hi, selmap, preferred_element_type=F32)
           + jnp.dot(lo, selmap, preferred_element_type=F32))
    blk = lax.broadcasted_iota(jnp.int32, (Q_BLOCK, n_sel), 1)
    qrow = q0 + lax.broadcasted_iota(jnp.int32, (Q_BLOCK, n_sel), 0)
    qblk = qrow >> 6
    forced = (blk == 0) | (blk == qblk) | (blk == qblk - 1)
    score = jnp.where(blk * SEL_LEN <= qrow, imp + jnp.where(forced, FORCE_BONUS, 0.0), NEG_INF)
    blk_f = blk.astype(F32)

    def pick(_, carry):
        sc_, sel_ = carry
        mx = jnp.max(sc_, axis=-1, keepdims=True)
        first = jnp.min(jnp.where(sc_ == mx, blk_f, float(n_sel)), axis=-1, keepdims=True)
        hit = blk_f == first
        return jnp.where(hit, PICKED, sc_), jnp.where(hit, 1.0, sel_)

    _, sel = lax.fori_loop(0, n_top, pick, (score, jnp.zeros((Q_BLOCK, n_sel), F32)))
    sel_b = sel.astype(BF16)

    m_sc[...] = jnp.full_like(m_sc, NEG_INF)
    l_sc[...] = jnp.zeros_like(l_sc)
    acc_sc[...] = jnp.zeros_like(acc_sc)
    blocks_per_tile = SEL_TILE // SEL_LEN
    e_row = lax.broadcasted_iota(jnp.int32, (n_sel, SEL_TILE), 0)
    e_col = lax.broadcasted_iota(jnp.int32, (n_sel, SEL_TILE), 1) >> 6
    k_iota = lax.broadcasted_iota(jnp.int32, (Q_BLOCK, SEL_TILE), 1)
    q_iota = q0 + lax.broadcasted_iota(jnp.int32, (Q_BLOCK, SEL_TILE), 0)

    def sel_tile(t, carry):
        k0 = pl.multiple_of(t * SEL_TILE, SEL_TILE)
        kv = kvs_ref[pl.ds(k0, SEL_TILE), :]
        expand = jnp.where(e_row == e_col + t * blocks_per_tile, 1.0, 0.0).astype(BF16)
        chosen = jnp.dot(sel_b, expand, preferred_element_type=F32)
        rel = (k0 + k_iota) - q_iota
        ok = (chosen > 0.5) & (rel <= 0)
        s = lax.dot_general(q, kv, NT_DIMS, preferred_element_type=F32) * SCALE
        s3 = (s.reshape(NSA_HG, Q_BLOCK, SEL_TILE) + slope3 * rel.astype(F32)[None]
              + jnp.where(ok, 0.0, NEG_INF)[None])
        s = s3.reshape(NSA_ROWS, SEL_TILE)
        m_prev = m_sc[...]
        m_new = jnp.maximum(m_prev, jnp.max(s, axis=-1, keepdims=True))
        alpha = jnp.exp(m_prev - m_new)
        p = jnp.exp(s - m_new)
        l_sc[...] = alpha * l_sc[...] + jnp.sum(p, axis=-1, keepdims=True)
        acc_sc[...] = alpha * acc_sc[...] + jnp.dot(p.astype(BF16), kv, preferred_element_type=F32)
        m_sc[...] = m_new
        return carry

    lax.fori_loop(0, (q0 + Q_BLOCK + SEL_TILE - 1) // SEL_TILE, sel_tile, 0)
    o_s = acc_sc[...] / l_sc[...]

    band = WIN + Q_BLOCK
    w0 = pl.multiple_of(jnp.maximum(q0 - WIN, 0), Q_BLOCK)
    kvw = kvw_ref[pl.ds(w0, band), :]
    rel_w = (w0 + lax.broadcasted_iota(jnp.int32, (Q_BLOCK, band), 1)
             - (q0 + lax.broadcasted_iota(jnp.int32, (Q_BLOCK, band), 0)))
    ok_w = (rel_w <= 0) & (rel_w > -WIN)
    sw = lax.dot_general(q, kvw, NT_DIMS, preferred_element_type=F32) * SCALE
    sw3 = (sw.reshape(NSA_HG, Q_BLOCK, band) + slope3 * rel_w.astype(F32)[None]
           + jnp.where(ok_w, 0.0, NEG_INF)[None])
    sw = sw3.reshape(NSA_ROWS, band)
    pw = jnp.exp(sw - jnp.max(sw, axis=-1, keepdims=True))
    o_w = (jnp.dot(pw.astype(BF16), kvw, preferred_element_type=F32)
           / jnp.sum(pw, axis=-1, keepdims=True))

    gate = _sigmoid(gl_ref[...])
    out = gate[:, 0:1] * o_c + gate[:, 1:2] * o_s + gate[:, 2:3] * o_w
    o_ref[...] = out[:, :HEAD_DIM].astype(o_ref.dtype)


def nsa_attention(qp, kvc, kvs, kvw, gate_logit, selmap_t):
    b, g, nb = qp.shape[:3]
    s = kvs.shape[2]
    nc = kvc.shape[2]
    n_sel = s // SEL_LEN
    assert s % SEL_TILE == 0 and s >= WIN + Q_BLOCK
    per_block = lambda width: pl.BlockSpec((None, None, None, NSA_ROWS, width),
                                           lambda b_, g_, i: (b_, g_, i, 0, 0))
    per_group = lambda rows: pl.BlockSpec((None, None, rows, LANES), lambda b_, g_, i: (b_, g_, 0, 0))
    return pl.pallas_call(
        functools.partial(_nsa_kernel, seq=s, n_top=min(SEL_TOP, n_sel)),
        out_shape=jax.ShapeDtypeStruct((b, g, nb, NSA_ROWS, HEAD_DIM), BF16),
        grid=(b, g, nb),
        in_specs=[per_block(LANES), per_group(nc), per_group(s), per_group(s), per_block(3),
                  pl.BlockSpec((nc, n_sel), lambda b_, g_, i: (0, 0))],
        out_specs=per_block(HEAD_DIM),
        scratch_shapes=[pltpu.VMEM((NSA_ROWS, 1), F32), pltpu.VMEM((NSA_ROWS, 1), F32),
                        pltpu.VMEM((NSA_ROWS, LANES), F32)],
        compiler_params=_cparams(3), name="nsa_attention",
    )(qp, kvc, kvs, kvw, gate_logit, selmap_t)


def _causal_conv3(u, halo, w):
    row = lax.broadcasted_iota(jnp.int32, u.shape, 0)
    last = halo.shape[0] - 1
    prev1 = halo[last:last + 1]
    prev2 = halo[last - 1:last]
    u1 = jnp.where(row == 0, prev1, pltpu.roll(u, 1, 0))
    u2 = jnp.where(row == 0, prev2, jnp.where(row == 1, prev1, pltpu.roll(u, 2, 0)))
    return w[0:1] * u2 + w[1:2] * u1 + w[2:3] * u


def _merge_kernel(ga_ref, gb_ref, gc_ref, cb_ref, cc_ref, cx_ref, cch_ref, cxh_ref, a_ref, c_ref,
                  wa_ref, wb_ref, wc_ref, cw_ref, o_ref, *, tm, seq):
    seq_start = (pl.program_id(0) * tm) % seq == 0
    u = cc_ref[...] * cx_ref[...]
    halo = jnp.where(seq_start, 0.0, cch_ref[...] * cxh_ref[...])
    conv = _causal_conv3(u, halo, cw_ref[...])
    y_a = jnp.dot(a_ref[...], wa_ref[...], preferred_element_type=F32)
    y_b = jnp.dot((cb_ref[...] * conv).astype(BF16), wb_ref[...], preferred_element_type=F32)
    y_c = jnp.dot(c_ref[...], wc_ref[...], preferred_element_type=F32)
    mix = _sigmoid(ga_ref[...]) * y_a + _sigmoid(gb_ref[...]) * y_b + _sigmoid(gc_ref[...]) * y_c
    o_ref[...] = mix.astype(o_ref.dtype)


def merge_branches(zf, a_out, c_out, w_a, w_b, w_c, conv_w, seq, tm=256):
    n = zf.shape[0]
    cw, d = w_b.shape
    tm = min(tm, n)
    gcol = lambda c: pl.BlockSpec((tm, d), lambda i: (i, c))
    ccol = lambda c: pl.BlockSpec((tm, cw), lambda i: (i, 3 * d // cw + c))
    hcol = lambda c: pl.BlockSpec((SUBLANES, cw),
                                  lambda i: (jnp.maximum(i * (tm // SUBLANES) - 1, 0), 3 * d // cw + c))
    row = lambda width: pl.BlockSpec((tm, width), lambda i: (i, 0))
    full = lambda shape: pl.BlockSpec(shape, lambda i: (0, 0))
    return pl.pallas_call(
        functools.partial(_merge_kernel, tm=tm, seq=seq),
        out_shape=jax.ShapeDtypeStruct((n, d), BF16),
        grid=(n // tm,),
        in_specs=[gcol(0), gcol(1), gcol(2), ccol(0), ccol(1), ccol(2), hcol(1), hcol(2),
                  row(a_out.shape[1]), row(c_out.shape[1]),
                  full(w_a.shape), full(w_b.shape), full(w_c.shape), full(conv_w.shape)],
        out_specs=pl.BlockSpec((tm, d), lambda i: (i, 0)),
        compiler_params=_cparams(1), name="merge_branches",
    )(zf, zf, zf, zf, zf, zf, zf, zf, a_out, c_out, w_a, w_b, w_c, conv_w)


def _ffn_up_kernel(h_ref, hh_ref, wu_ref, wv_ref, cw_ref, o_ref, *, tm, seq):
    seq_start = (pl.program_id(1) * tm) % seq == 0
    wu = wu_ref[...]
    u = jnp.dot(h_ref[...], wu, preferred_element_type=F32)
    v = jnp.dot(h_ref[...], wv_ref[...], preferred_element_type=F32)
    halo = jnp.where(seq_start, 0.0, jnp.dot(hh_ref[...], wu, preferred_element_type=F32))
    c = _causal_conv3(u, halo, cw_ref[...])
    o_ref[...] = (c * _sigmoid(c) * v).astype(o_ref.dtype)


def ffn_up(h, w_up, conv_w, seq, tm=512):
    n, d = h.shape
    f = conv_w.shape[1]
    tm = min(tm, n)
    nj = 2 if (f // 2) % LANES == 0 else 1
    tn = f // nj
    return pl.pallas_call(
        functools.partial(_ffn_up_kernel, tm=tm, seq=seq),
        out_shape=jax.ShapeDtypeStruct((n, f), BF16),
        grid=(nj, n // tm),
        in_specs=[pl.BlockSpec((tm, d), lambda j, i: (i, 0)),
                  pl.BlockSpec((2 * SUBLANES, d), lambda j, i: (jnp.maximum(i * (tm // (2 * SUBLANES)) - 1, 0), 0)),
                  pl.BlockSpec((d, tn), lambda j, i: (0, j)),
                  pl.BlockSpec((d, tn), lambda j, i: (0, nj + j)),
                  pl.BlockSpec((conv_w.shape[0], tn), lambda j, i: (0, j))],
        out_specs=pl.BlockSpec((tm, tn), lambda j, i: (i, j)),
        compiler_params=_cparams(2), name="ffn_up",
    )(h, h, w_up, w_up, conv_w)


def _selection_map_t(n_sel, n_cmp_pad):
    sel_start = np.arange(n_sel)[None, :] * SEL_LEN
    cmp_start = np.arange(n_cmp_pad)[:, None] * CMP_STRIDE
    ov = np.minimum(sel_start + SEL_LEN, cmp_start + CMP_LEN) - np.maximum(sel_start, cmp_start)
    ov = np.clip(ov, 0, None) / CMP_STRIDE
    ov[n_cmp_pad - 1] = 0.0
    return ov.astype(np.float32)


def _heads_major(t, heads):
    b, s, _ = t.shape
    return t.reshape(b, s, heads, HEAD_DIM).transpose(0, 2, 1, 3)


def _overlapping_blocks(t):
    b, s, _ = t.shape
    chunks = _heads_major(t, NSA_KV_GROUPS).reshape(b, NSA_KV_GROUPS, s // CMP_STRIDE, CMP_STRIDE * HEAD_DIM)
    nxt = jnp.concatenate([chunks[:, :, 1:], jnp.zeros_like(chunks[:, :, :1])], axis=2)
    return jnp.concatenate([chunks, nxt], axis=-1)


def _split_w_in(w_in_l, b_in_l, d):
    sizes = (d, d, d, 512, 512, 512, FOX_HEADS, 512, 512, 512, 512, 128, 128, 128, 128, 128, 128, NSA_HEADS * 3)
    offs = np.concatenate([[0], np.cumsum(sizes)])
    col = lambda a, k: a[..., int(offs[k]):int(offs[k + 1])]
    (GA, GB, GC, FQ, FK, FV, FF, CB, CC, CX, NQ, NKC, NVC, NKS, NVS, NKW, NVW, NG) = range(18)
    pad = 384 - (128 + 128 + FOX_HEADS + NSA_HEADS * 3)

    def group(idx, extra=0):
        w = jnp.concatenate([col(w_in_l, k) for k in idx], axis=-1)
        bias = jnp.concatenate([col(b_in_l, k) for k in idx], axis=-1)
        if extra:
            w = jnp.pad(w, ((0, 0), (0, extra)))
            bias = jnp.pad(bias, ((0, extra),))
        return w.astype(BF16), bias

    return (group((FQ, FK, FV, NQ, NKS, NVS, NKW, NVW)),
            group((GA, GB, GC, CB, CC, CX)),
            group((NKC, NVC, FF, NG), pad))


def kernel(x, norm_mix_g, w_in, b_in, cmp_pe_k, cmp_w1_k, cmp_w2_k, cmp_pe_v, cmp_w1_v, cmp_w2_v, sc_conv_w, w_br_a, w_br_b, w_br_c, w_o, norm_ffn_g, w_up, ffn_conv_w, w_down, norm_final_g):
    b, s, d = x.shape
    depth = w_in.shape[0]
    n = b * s
    nqb = s // Q_BLOCK
    n_cmp_pad = s // CMP_STRIDE
    selmap_t = jnp.asarray(_selection_map_t(s // SEL_LEN, n_cmp_pad), dtype=BF16)

    xr = x.reshape(n, d)
    h = rmsnorm_rows(xr, norm_mix_g[0], BF16)
    for l in range(depth):
        (w_b16, b_b16), (w_f32, b_f32), (w_sm, b_sm) = _split_w_in(w_in[l], b_in[l], d)
        zb = matmul_bias(h, w_b16, b_b16, BF16, tn=512)
        zf = matmul_bias(h, w_f32, b_f32, F32, tn=512)
        zs = matmul_bias(h, w_sm, b_sm, F32, tn=384)
        zb3 = zb.reshape(b, s, -1)
        zs3 = zs.reshape(b, s, -1)

        fq = _heads_major(zb3[..., 0:512], FOX_HEADS)
        fk = _heads_major(zb3[..., 512:1024], FOX_HEADS)
        fv = _heads_major(zb3[..., 1024:1536], FOX_HEADS)
        cum = logsig_cumsum(zs3[..., 256:256 + FOX_HEADS].transpose(0, 2, 1))
        a_out = fox_attention(jnp.concatenate([jnp.zeros_like(fq), fq], axis=-1),
                              jnp.concatenate([fv, fk], axis=-1),
                              cum[..., None], cum[:, :, None, :])
        a_out = a_out.transpose(0, 2, 1, 3).reshape(n, FOX_HEADS * HEAD_DIM)

        kvc = compress_blocks(_overlapping_blocks(zs3[..., 0:128]), _overlapping_blocks(zs3[..., 128:256]),
                              cmp_pe_k[l].reshape(1, -1), cmp_pe_v[l].reshape(1, -1),
                              cmp_w1_k[l].astype(BF16), cmp_w2_k[l].astype(BF16),
                              cmp_w1_v[l].astype(BF16), cmp_w2_v[l].astype(BF16))
        nq = zb3[..., 1536:2048].reshape(b, nqb, Q_BLOCK, NSA_KV_GROUPS, NSA_HG, HEAD_DIM)
        nq = nq.transpose(0, 3, 1, 4, 2, 5).reshape(b, NSA_KV_GROUPS, nqb, NSA_ROWS, HEAD_DIM)
        nqp = jnp.concatenate([jnp.zeros_like(nq), nq], axis=-1)
        kvs = jnp.concatenate([_heads_major(zb3[..., 2176:2304], NSA_KV_GROUPS),
                               _heads_major(zb3[..., 2048:2176], NSA_KV_GROUPS)], axis=-1)
        kvw = jnp.concatenate([_heads_major(zb3[..., 2432:2560], NSA_KV_GROUPS),
                               _heads_major(zb3[..., 2304:2432], NSA_KV_GROUPS)], axis=-1)
        ng = zs3[..., 264:264 + NSA_HEADS * 3].reshape(b, nqb, Q_BLOCK, NSA_KV_GROUPS, NSA_HG, 3)
        ng = ng.transpose(0, 3, 1, 4, 2, 5).reshape(b, NSA_KV_GROUPS, nqb, NSA_ROWS, 3)
        c_out = nsa_attention(nqp, kvc, kvs, kvw, ng, selmap_t)
        c_out = c_out.reshape(b, NSA_KV_GROUPS, nqb, NSA_HG, Q_BLOCK, HEAD_DIM)
        c_out = c_out.transpose(0, 2, 4, 1, 3, 5).reshape(n, NSA_HEADS * HEAD_DIM)

        mix = merge_branches(zf, a_out, c_out, w_br_a[l].astype(BF16), w_br_b[l].astype(BF16),
                             w_br_c[l].astype(BF16), sc_conv_w[l], s)
        xr, h2 = residual_matmul_norm(mix, w_o[l].astype(BF16), xr, norm_ffn_g[l], BF16)

        act = ffn_up(h2, w_up[l].astype(BF16), ffn_conv_w[l], s)
        if l + 1 < depth:
            xr, h = residual_matmul_norm(act, w_down[l].astype(BF16), xr, norm_mix_g[l + 1], BF16)
        else:
            out = residual_matmul_norm(act, w_down[l].astype(BF16), xr, norm_final_g, F32,
                                       keep_residual=False)
    return out.reshape(b, s, d)
```

```python
import functools

import numpy as np
import jax
import jax.numpy as jnp
from jax import lax
from jax.experimental import pallas as pl
from jax.experimental.pallas import tpu as pltpu

F32 = jnp.float32
BF16 = jnp.bfloat16

HEAD_DIM = 64
FOX_HEADS = 8
NSA_HEADS = 8
NSA_KV_GROUPS = 2
NSA_HG = NSA_HEADS // NSA_KV_GROUPS
CMP_LEN = 32
CMP_STRIDE = 16
SEL_LEN = 64
SEL_TOP = 16
WIN = 512
Q_BLOCK = 128
EPS = 1e-6
NEG_INF = -1e30
FORCE_BONUS = 1e4
SCALE = HEAD_DIM ** -0.5

LANES = 128
SUBLANES = 8
VMEM_LIMIT = 48 * 1024 * 1024

NT_DIMS = (((1,), (1,)), ((), ()))


def _cparams(n_parallel, n_arbitrary=0):
    return pltpu.CompilerParams(
        dimension_semantics=("parallel",) * n_parallel + ("arbitrary",) * n_arbitrary,
        vmem_limit_bytes=VMEM_LIMIT)


def _sigmoid(x):
    return 1.0 / (1.0 + jnp.exp(-x))


def _rms(x, g):
    ms = jnp.mean(x * x, axis=-1, keepdims=True)
    return x * lax.rsqrt(ms + EPS) * g


def _norm_kernel(x_ref, g_ref, o_ref):
    o_ref[...] = _rms(x_ref[...], g_ref[...]).astype(o_ref.dtype)


def rmsnorm_rows(x, g, out_dtype, tm=1024):
    n, d = x.shape
    tm = min(tm, n)
    return pl.pallas_call(
        _norm_kernel,
        out_shape=jax.ShapeDtypeStruct((n, d), out_dtype),
        grid=(n // tm,),
        in_specs=[pl.BlockSpec((tm, d), lambda i: (i, 0)),
                  pl.BlockSpec((1, d), lambda i: (0, 0))],
        out_specs=pl.BlockSpec((tm, d), lambda i: (i, 0)),
        compiler_params=_cparams(1),
        name="rmsnorm",
    )(x, g.reshape(1, d))


def _mm_bias_kernel(a_ref, w_ref, b_ref, o_ref):
    acc = jnp.dot(a_ref[...], w_ref[...], preferred_element_type=F32)
    o_ref[...] = (acc + b_ref[...]).astype(o_ref.dtype)


def matmul_bias(a, w, b, out_dtype, tn, tm=1024):
    n, k = a.shape
    m = w.shape[1]
    tm = min(tm, n)
    return pl.pallas_call(
        _mm_bias_kernel,
        out_shape=jax.ShapeDtypeStruct((n, m), out_dtype),
        grid=(n // tm, m // tn),
        in_specs=[pl.BlockSpec((tm, k), lambda i, j: (i, 0)),
                  pl.BlockSpec((k, tn), lambda i, j: (0, j)),
                  pl.BlockSpec((1, tn), lambda i, j: (0, j))],
        out_specs=pl.BlockSpec((tm, tn), lambda i, j: (i, j)),
        compiler_params=_cparams(2),
        name="matmul_bias",
    )(a, w, b.reshape(1, m))


def _res_norm_kernel(a_ref, w_ref, x_ref, g_ref, xo_ref, ho_ref):
    xn = x_ref[...] + jnp.dot(a_ref[...], w_ref[...], preferred_element_type=F32)
    xo_ref[...] = xn
    ho_ref[...] = _rms(xn, g_ref[...]).astype(ho_ref.dtype)


def _res_norm_only_kernel(a_ref, w_ref, x_ref, g_ref, ho_ref):
    xn = x_ref[...] + jnp.dot(a_ref[...], w_ref[...], preferred_element_type=F32)
    ho_ref[...] = _rms(xn, g_ref[...]).astype(ho_ref.dtype)


def residual_matmul_norm(a, w, x, g, norm_dtype, keep_residual=True, tm=512):
    n, k = a.shape
    d = w.shape[1]
    tm = min(tm, n)
    row = lambda i: (i, 0)
    in_specs = [pl.BlockSpec((tm, k), row),
                pl.BlockSpec((k, d), lambda i: (0, 0)),
                pl.BlockSpec((tm, d), row),
                pl.BlockSpec((1, d), lambda i: (0, 0))]
    if keep_residual:
        return pl.pallas_call(
            _res_norm_kernel,
            out_shape=(jax.ShapeDtypeStruct((n, d), F32), jax.ShapeDtypeStruct((n, d), norm_dtype)),
            grid=(n // tm,), in_specs=in_specs,
            out_specs=(pl.BlockSpec((tm, d), row), pl.BlockSpec((tm, d), row)),
            compiler_params=_cparams(1), name="residual_matmul_norm",
        )(a, w, x, g.reshape(1, d))
    return pl.pallas_call(
        _res_norm_only_kernel,
        out_shape=jax.ShapeDtypeStruct((n, d), norm_dtype),
        grid=(n // tm,), in_specs=in_specs,
        out_specs=pl.BlockSpec((tm, d), row),
        compiler_params=_cparams(1), name="residual_matmul_final_norm",
    )(a, w, x, g.reshape(1, d))


CUM_BLOCK = 256


def _bf16_pieces(x):
    hi = x.astype(BF16).astype(F32)
    r1 = x - hi
    mid = r1.astype(BF16).astype(F32)
    lo = (r1 - mid).astype(BF16).astype(F32)
    return hi, mid, lo


def _logsig_cumsum_kernel(f_ref, o_ref):
    h, s = f_ref.shape
    x = f_ref[...]
    ls = jnp.minimum(x, 0.0) - jnp.log1p(jnp.exp(-jnp.abs(x)))
    r = lax.broadcasted_iota(jnp.int32, (CUM_BLOCK, CUM_BLOCK), 0)
    c = lax.broadcasted_iota(jnp.int32, (CUM_BLOCK, CUM_BLOCK), 1)
    tri = jnp.where(r <= c, 1.0, 0.0).astype(BF16)
    carry = jnp.zeros((h, 1), F32)
    for blk in range(s // CUM_BLOCK):
        cols = slice(blk * CUM_BLOCK, (blk + 1) * CUM_BLOCK)
        cs = carry
        for piece in _bf16_pieces(ls[:, cols]):
            cs = cs + jnp.dot(piece.astype(BF16), tri, preferred_element_type=F32)
        carry = cs[:, CUM_BLOCK - 1:CUM_BLOCK]
        for k, piece in enumerate(_bf16_pieces(cs)):
            o_ref[k, :, cols] = piece
            o_ref[3 + k, :, cols] = -piece


def logsig_cumsum(f_t):
    b, h, s = f_t.shape
    return pl.pallas_call(
        _logsig_cumsum_kernel,
        out_shape=jax.ShapeDtypeStruct((b, 6, h, s), F32),
        grid=(b,),
        in_specs=[pl.BlockSpec((None, h, s), lambda i: (i, 0, 0))],
        out_specs=pl.BlockSpec((None, 6, h, s), lambda i: (i, 0, 0, 0)),
        compiler_params=_cparams(1), name="logsig_cumsum",
    )(f_t)


KEY_TILE = 512
ATTN_ROWS = 512
SOFTMAX_ROWS = 64


def _online_softmax_tiles(n_full, logits, values, last_bias, s_sc, p_sc, m_sc, a_sc, acc_sc):
    rows_total, tile = p_sc.shape
    rep = tile // LANES
    m_sc[...] = jnp.full_like(m_sc, NEG_INF)
    acc_sc[...] = jnp.zeros_like(acc_sc)

    def update(slot, t, last):
        for c in range(rows_total // SOFTMAX_ROWS):
            rows = pl.ds(c * SOFTMAX_ROWS, SOFTMAX_ROWS)
            s = s_sc[slot, rows, :]
            if last:
                s = last_bias(s, c * SOFTMAX_ROWS)
            m_prev = m_sc[rows, :]
            m_new = jnp.maximum(m_prev, jnp.max(s, axis=-1, keepdims=True))
            a_sc[rows, :] = jnp.exp(m_prev - m_new)
            m_sc[rows, :] = m_new
            p_sc[rows, :] = jnp.exp(s - jnp.concatenate([m_new] * rep, axis=1)).astype(BF16)
        acc_sc[...] = a_sc[...] * acc_sc[...] + jnp.dot(p_sc[...], values(t), preferred_element_type=F32)

    s_sc[0] = logits(0)

    def two_tiles(u, carry):
        t = 2 * u
        s_sc[1] = logits(t + 1)
        update(0, t, False)
        s_sc[0] = logits(t + 2)
        update(1, t + 1, False)
        return carry

    lax.fori_loop(0, n_full // 2, two_tiles, 0)

    @pl.when(n_full % 2 == 1)
    def _():
        s_sc[1] = logits(n_full)
        update(0, n_full - 1, False)
        update(1, n_full, True)

    @pl.when(n_full % 2 == 0)
    def _():
        update(0, n_full, True)


def _softmax_scratch(rows, tile):
    return [pltpu.VMEM((2, rows, tile), F32), pltpu.VMEM((rows, tile), BF16),
            pltpu.VMEM((rows, LANES), F32), pltpu.VMEM((rows, LANES), F32), pltpu.VMEM((rows, LANES), F32)]


def _normalized(acc):
    return acc[:, :HEAD_DIM] / acc[:, HEAD_DIM:HEAD_DIM + 1]


def _fox_kernel(q_ref, k_ref, v_ref, o_ref, s_sc, p_sc, m_sc, a_sc, acc_sc):
    tile = KEY_TILE
    i = pl.program_id(2)
    q = q_ref[...]

    def logits(t):
        k0 = pl.multiple_of(t * tile, tile)
        return lax.dot_general(q, k_ref[pl.ds(k0, tile), :], NT_DIMS, preferred_element_type=F32)

    def values(t):
        return v_ref[pl.ds(pl.multiple_of(t * tile, tile), tile), :]

    def causal(s, first_row):
        qi = first_row + lax.broadcasted_iota(jnp.int32, s.shape, 0)
        ki = lax.broadcasted_iota(jnp.int32, s.shape, 1)
        return jnp.where(ki <= qi, s, NEG_INF)

    _online_softmax_tiles(i, logits, values, causal, s_sc, p_sc, m_sc, a_sc, acc_sc)
    o_ref[...] = _normalized(acc_sc[...]).astype(o_ref.dtype)


def fox_attention(q_aug, k_aug, v_aug):
    b, h, s, _ = q_aug.shape
    tile = KEY_TILE
    assert s % tile == 0 and ATTN_ROWS == tile
    per_head = pl.BlockSpec((None, None, s, LANES), lambda b_, h_, i: (b_, h_, 0, 0))
    return pl.pallas_call(
        _fox_kernel,
        out_shape=jax.ShapeDtypeStruct((b, h, s, HEAD_DIM), BF16),
        grid=(b, h, s // tile),
        in_specs=[pl.BlockSpec((None, None, tile, LANES), lambda b_, h_, i: (b_, h_, i, 0)),
                  per_head, per_head],
        out_specs=pl.BlockSpec((None, None, tile, HEAD_DIM), lambda b_, h_, i: (b_, h_, i, 0)),
        scratch_shapes=_softmax_scratch(ATTN_ROWS, tile),
        compiler_params=_cparams(3), name="fox_attention",
    )(q_aug, k_aug, v_aug)


def _gelu_tanh(x):
    return 0.5 * x * (1.0 + jnp.tanh(np.sqrt(2.0 / np.pi).astype(np.float32) * (x + 0.044715 * (x * x * x))))


def _compress_kernel(bk_ref, bv_ref, pek_ref, pev_ref, w1k_ref, w2k_ref, w1v_ref, w2v_ref, o_ref):
    def mlp(blk_ref, pe_ref, w1_ref, w2_ref):
        xin = (blk_ref[...] + pe_ref[...]).astype(BF16)
        hid = _gelu_tanh(jnp.dot(xin, w1_ref[...], preferred_element_type=F32))
        return jnp.dot(hid.astype(BF16), w2_ref[...], preferred_element_type=F32)

    k_cmp = mlp(bk_ref, pek_ref, w1k_ref, w2k_ref)
    v_cmp = mlp(bv_ref, pev_ref, w1v_ref, w2v_ref)
    o_ref[...] = jnp.concatenate([v_cmp, k_cmp], axis=-1).astype(o_ref.dtype)


def compress_blocks(blk_k, blk_v, pe_k, pe_v, w1_k, w2_k, w1_v, w2_v):
    b, g, nc, feat = blk_k.shape
    blk_spec = pl.BlockSpec((None, None, nc, feat), lambda i, j: (i, j, 0, 0))
    full = lambda shape: pl.BlockSpec(shape, lambda i, j: (0,) * len(shape))
    return pl.pallas_call(
        _compress_kernel,
        out_shape=jax.ShapeDtypeStruct((b, g, nc, LANES), BF16),
        grid=(b, g),
        in_specs=[blk_spec, blk_spec, full((1, feat)), full((1, feat)),
                  full((feat, HEAD_DIM)), full((HEAD_DIM, HEAD_DIM)),
                  full((feat, HEAD_DIM)), full((HEAD_DIM, HEAD_DIM))],
        out_specs=pl.BlockSpec((None, None, nc, LANES), lambda i, j: (i, j, 0, 0)),
        compiler_params=_cparams(2), name="nsa_compress",
    )(blk_k, blk_v, pe_k, pe_v, w1_k, w2_k, w1_v, w2_v)


NSA_ROWS = NSA_HG * Q_BLOCK


def _select_blocks_t(score_ref, n_sel, n_top):
    groups = n_sel // SUBLANES
    grp = [score_ref[pl.ds(g * SUBLANES, SUBLANES), :] for g in range(groups)]
    blk = lax.broadcasted_iota(jnp.int32, (SUBLANES, LANES), 0)
    beaten_by = [jnp.zeros((SUBLANES, LANES), F32) for _ in range(groups)]
    for k in range(n_sel):
        sk = jnp.broadcast_to(score_ref[pl.ds(k, 1), :], (SUBLANES, LANES))
        for g in range(groups):
            if g * SUBLANES > k:
                wins = sk >= grp[g]
            elif g * SUBLANES + SUBLANES - 1 < k:
                wins = sk > grp[g]
            else:
                wins = (sk > grp[g]) | ((sk == grp[g]) & (blk + g * SUBLANES > k))
            beaten_by[g] = beaten_by[g] + jnp.where(wins, 1.0, 0.0)
    return jnp.concatenate([jnp.where(c < n_top, 1.0, 0.0) for c in beaten_by], axis=0)


def _nsa_kernel(q_ref, kc_ref, vc_ref, ks_ref, vs_ref, kw_ref, vw_ref, gl_ref, selmap_ref, o_ref,
                score_sc, qa_sc, s_sc, p_sc, m_sc, a_sc, acc_sc, *, seq, n_top):
    i = pl.program_id(2)
    q0 = i * Q_BLOCK
    n_cmp_pad = seq // CMP_STRIDE
    n_sel = seq // SEL_LEN
    q1 = q_ref[...]
    row = lax.broadcasted_iota(jnp.int32, (NSA_ROWS, 1), 0)
    qpos_col = q0 + (row & (Q_BLOCK - 1))

    cend = lax.broadcasted_iota(jnp.int32, (1, n_cmp_pad), 1) * CMP_STRIDE + (CMP_LEN - 1)
    valid_c = cend <= qpos_col
    lc = jnp.where(valid_c, lax.dot_general(q1, kc_ref[...], NT_DIMS, preferred_element_type=F32), NEG_INF)
    mc = jnp.max(lc, axis=-1, keepdims=True)
    pc = jnp.where(valid_c, jnp.exp(lc - mc), 0.0)
    sc = jnp.sum(pc, axis=-1, keepdims=True)
    pc = pc / jnp.where(sc > 0, sc, 1.0)
    o_c = jnp.dot(pc.astype(BF16), vc_ref[...], preferred_element_type=F32)[:, :HEAD_DIM]

    pcs = pc[0:Q_BLOCK]
    for hh in range(1, NSA_HG):
        pcs = pcs + pc[hh * Q_BLOCK:(hh + 1) * Q_BLOCK]
    hi = pcs.astype(BF16)
    lo = (pcs - hi.astype(F32)).astype(BF16)
    selmap = selmap_ref[...]
    imp_t = (lax.dot_general(selmap, hi, NT_DIMS, preferred_element_type=F32)
             + lax.dot_general(selmap, lo, NT_DIMS, preferred_element_type=F32))
    blk = lax.broadcasted_iota(jnp.int32, (n_sel, Q_BLOCK), 0)
    qrow = q0 + lax.broadcasted_iota(jnp.int32, (n_sel, Q_BLOCK), 1)
    qblk = qrow >> 6
    forced = (blk == 0) | (blk == qblk) | (blk == qblk - 1)
    score_sc[...] = jnp.where(blk * SEL_LEN <= qrow, imp_t + jnp.where(forced, FORCE_BONUS, 0.0), NEG_INF)
    sel_t = _select_blocks_t(score_sc, n_sel, n_top)

    sel_t = jnp.concatenate([sel_t, jnp.ones((LANES - n_sel, Q_BLOCK), F32)], axis=0).astype(BF16)
    eye = jnp.where(lax.broadcasted_iota(jnp.int32, (Q_BLOCK, Q_BLOCK), 0)
                    == lax.broadcasted_iota(jnp.int32, (Q_BLOCK, Q_BLOCK), 1), 1.0, 0.0).astype(BF16)
    sel = lax.dot_general(eye, sel_t, NT_DIMS, preferred_element_type=F32)
    closed = jnp.where(sel > 0.5, 0.0, NEG_INF).astype(BF16)
    qa_sc[:, 0:LANES] = q1
    for hh in range(NSA_HG):
        qa_sc[hh * Q_BLOCK:(hh + 1) * Q_BLOCK, LANES:2 * LANES] = closed

    tile = KEY_TILE
    n_full = q0 // tile

    def logits(t):
        k0 = pl.multiple_of(t * tile, tile)
        return lax.dot_general(qa_sc[...], ks_ref[pl.ds(k0, tile), :], NT_DIMS, preferred_element_type=F32)

    def values(t):
        return vs_ref[pl.ds(pl.multiple_of(t * tile, tile), tile), :]

    def causal(s, first_row):
        r = first_row + lax.broadcasted_iota(jnp.int32, s.shape, 0)
        kpos = n_full * tile + lax.broadcasted_iota(jnp.int32, s.shape, 1)
        return jnp.where(kpos <= q0 + (r & (Q_BLOCK - 1)), s, NEG_INF)

    _online_softmax_tiles(n_full, logits, values, causal, s_sc, p_sc, m_sc, a_sc, acc_sc)
    o_s = _normalized(acc_sc[...])

    band = WIN + Q_BLOCK
    w0 = pl.multiple_of(jnp.maximum(q0 - WIN, 0), Q_BLOCK)
    rel_w = (w0 + lax.broadcasted_iota(jnp.int32, (Q_BLOCK, band), 1)
             - (q0 + lax.broadcasted_iota(jnp.int32, (Q_BLOCK, band), 0)))
    bias_w = jnp.where((rel_w <= 0) & (rel_w > -WIN), 0.0, NEG_INF)
    sw = lax.dot_general(q1, kw_ref[pl.ds(w0, band), :], NT_DIMS, preferred_element_type=F32)
    sw = (sw.reshape(NSA_HG, Q_BLOCK, band) + bias_w[None]).reshape(NSA_ROWS, band)
    pw = jnp.exp(sw - jnp.max(sw, axis=-1, keepdims=True))
    o_w = _normalized(jnp.dot(pw.astype(BF16), vw_ref[pl.ds(w0, band), :], preferred_element_type=F32))

    gate = _sigmoid(gl_ref[...])
    out = gate[:, 0:1] * o_c + gate[:, 1:2] * o_s + gate[:, 2:3] * o_w
    o_ref[...] = out.astype(o_ref.dtype)


def nsa_attention(q_aug, kc_aug, vc_aug, ks_aug, vs_aug, kw_aug, vw_aug, gate_logit, selmap):
    b, g, nb = q_aug.shape[:3]
    s = vs_aug.shape[2]
    nc = kc_aug.shape[2]
    n_sel = s // SEL_LEN
    assert s % KEY_TILE == 0 and s >= WIN + Q_BLOCK and n_sel <= LANES and NSA_ROWS == ATTN_ROWS
    per_block = lambda width: pl.BlockSpec((None, None, None, NSA_ROWS, width),
                                           lambda b_, g_, i: (b_, g_, i, 0, 0))
    per_group = lambda rows, width: pl.BlockSpec((None, None, rows, width), lambda b_, g_, i: (b_, g_, 0, 0))
    return pl.pallas_call(
        functools.partial(_nsa_kernel, seq=s, n_top=min(SEL_TOP, n_sel)),
        out_shape=jax.ShapeDtypeStruct((b, g, nb, NSA_ROWS, HEAD_DIM), BF16),
        grid=(b, g, nb),
        in_specs=[per_block(LANES), per_group(nc, LANES), per_group(nc, LANES),
                  per_group(s, 2 * LANES), per_group(s, LANES), per_group(s, LANES), per_group(s, LANES),
                  per_block(3), pl.BlockSpec((n_sel, nc), lambda b_, g_, i: (0, 0))],
        out_specs=per_block(HEAD_DIM),
        scratch_shapes=[pltpu.VMEM((n_sel, Q_BLOCK), F32), pltpu.VMEM((NSA_ROWS, 2 * LANES), BF16)]
        + _softmax_scratch(NSA_ROWS, KEY_TILE),
        compiler_params=_cparams(3), name="nsa_attention",
    )(q_aug, kc_aug, vc_aug, ks_aug, vs_aug, kw_aug, vw_aug, gate_logit, selmap)


def _causal_conv3(u, halo, w):
    row = lax.broadcasted_iota(jnp.int32, u.shape, 0)
    last = halo.shape[0] - 1
    prev1 = halo[last:last + 1]
    prev2 = halo[last - 1:last]
    u1 = jnp.where(row == 0, prev1, pltpu.roll(u, 1, 0))
    u2 = jnp.where(row == 0, prev2, jnp.where(row == 1, prev1, pltpu.roll(u, 2, 0)))
    return w[0:1] * u2 + w[1:2] * u1 + w[2:3] * u


def _merge_kernel(ga_ref, gb_ref, gc_ref, cb_ref, cc_ref, cx_ref, cch_ref, cxh_ref, a_ref, c_ref,
                  wa_ref, wb_ref, wc_ref, cw_ref, o_ref, *, tm, seq):
    seq_start = (pl.program_id(0) * tm) % seq == 0
    u = cc_ref[...] * cx_ref[...]
    halo = jnp.where(seq_start, 0.0, cch_ref[...] * cxh_ref[...])
    conv = _causal_conv3(u, halo, cw_ref[...])
    y_a = jnp.dot(a_ref[...], wa_ref[...], preferred_element_type=F32)
    y_b = jnp.dot((cb_ref[...] * conv).astype(BF16), wb_ref[...], preferred_element_type=F32)
    y_c = jnp.dot(c_ref[...], wc_ref[...], preferred_element_type=F32)
    mix = _sigmoid(ga_ref[...]) * y_a + _sigmoid(gb_ref[...]) * y_b + _sigmoid(gc_ref[...]) * y_c
    o_ref[...] = mix.astype(o_ref.dtype)


def merge_branches(zf, a_out, c_out, w_a, w_b, w_c, conv_w, seq, tm=256):
    n = zf.shape[0]
    cw, d = w_b.shape
    tm = min(tm, n)
    gcol = lambda c: pl.BlockSpec((tm, d), lambda i: (i, c))
    ccol = lambda c: pl.BlockSpec((tm, cw), lambda i: (i, 3 * d // cw + c))
    hcol = lambda c: pl.BlockSpec((SUBLANES, cw),
                                  lambda i: (jnp.maximum(i * (tm // SUBLANES) - 1, 0), 3 * d // cw + c))
    row = lambda width: pl.BlockSpec((tm, width), lambda i: (i, 0))
    full = lambda shape: pl.BlockSpec(shape, lambda i: (0, 0))
    return pl.pallas_call(
        functools.partial(_merge_kernel, tm=tm, seq=seq),
        out_shape=jax.ShapeDtypeStruct((n, d), BF16),
        grid=(n // tm,),
        in_specs=[gcol(0), gcol(1), gcol(2), ccol(0), ccol(1), ccol(2), hcol(1), hcol(2),
                  row(a_out.shape[1]), row(c_out.shape[1]),
                  full(w_a.shape), full(w_b.shape), full(w_c.shape), full(conv_w.shape)],
        out_specs=pl.BlockSpec((tm, d), lambda i: (i, 0)),
        compiler_params=_cparams(1), name="merge_branches",
    )(zf, zf, zf, zf, zf, zf, zf, zf, a_out, c_out, w_a, w_b, w_c, conv_w)


def _ffn_up_kernel(h_ref, hh_ref, wu_ref, wv_ref, cw_ref, o_ref, *, tm, seq):
    seq_start = (pl.program_id(1) * tm) % seq == 0
    wu = wu_ref[...]
    u = jnp.dot(h_ref[...], wu, preferred_element_type=F32)
    v = jnp.dot(h_ref[...], wv_ref[...], preferred_element_type=F32)
    halo = jnp.where(seq_start, 0.0, jnp.dot(hh_ref[...], wu, preferred_element_type=F32))
    c = _causal_conv3(u, halo, cw_ref[...])
    o_ref[...] = (c * _sigmoid(c) * v).astype(o_ref.dtype)


def ffn_up(h, w_up, conv_w, seq, tm=512):
    n, d = h.shape
    f = conv_w.shape[1]
    tm = min(tm, n)
    nj = 2 if (f // 2) % LANES == 0 else 1
    tn = f // nj
    return pl.pallas_call(
        functools.partial(_ffn_up_kernel, tm=tm, seq=seq),
        out_shape=jax.ShapeDtypeStruct((n, f), BF16),
        grid=(nj, n // tm),
        in_specs=[pl.BlockSpec((tm, d), lambda j, i: (i, 0)),
                  pl.BlockSpec((2 * SUBLANES, d), lambda j, i: (jnp.maximum(i * (tm // (2 * SUBLANES)) - 1, 0), 0)),
                  pl.BlockSpec((d, tn), lambda j, i: (0, j)),
                  pl.BlockSpec((d, tn), lambda j, i: (0, nj + j)),
                  pl.BlockSpec((conv_w.shape[0], tn), lambda j, i: (0, j))],
        out_specs=pl.BlockSpec((tm, tn), lambda j, i: (i, j)),
        compiler_params=_cparams(2), name="ffn_up",
    )(h, h, w_up, w_up, conv_w)


def _selection_map(n_sel, n_cmp_pad):
    sel_start = np.arange(n_sel)[:, None] * SEL_LEN
    cmp_start = np.arange(n_cmp_pad)[None, :] * CMP_STRIDE
    ov = np.minimum(sel_start + SEL_LEN, cmp_start + CMP_LEN) - np.maximum(sel_start, cmp_start)
    ov = np.clip(ov, 0, None) / CMP_STRIDE
    ov[:, n_cmp_pad - 1] = 0.0
    return ov.astype(np.float32)


def _bf16_exact(a):
    a = np.asarray(a, np.float32)
    assert np.array_equal(a, a.astype(BF16).astype(np.float32)), "constant is not exact in bf16"
    return a


def _alibi_query_cols(nqb):
    slope = np.array([2.0 ** (-8.0 * (i + 1) / NSA_HEADS) for i in range(NSA_HEADS)], np.float32)
    slope = slope.reshape(NSA_KV_GROUPS, 1, NSA_HG, 1)
    qpos = (np.arange(nqb)[:, None] * Q_BLOCK + np.arange(Q_BLOCK)[None, :]).reshape(1, nqb, 1, Q_BLOCK)
    cols = np.stack(np.broadcast_arrays(slope * SEL_LEN, slope, -slope * SEL_LEN * (qpos // SEL_LEN),
                                        -slope * (qpos % SEL_LEN)), axis=-1)
    return _bf16_exact(cols.reshape(NSA_KV_GROUPS, nqb, NSA_ROWS, 4))


def _alibi_key_cols(pos):
    pos = np.asarray(pos)
    return _bf16_exact(np.stack([pos // SEL_LEN, pos % SEL_LEN, np.ones_like(pos), np.ones_like(pos)], axis=-1))


def _append_cols(x, cols, width):
    cols = jnp.broadcast_to(jnp.asarray(cols, x.dtype), x.shape[:-1] + (cols.shape[-1],))
    pad = jnp.zeros(x.shape[:-1] + (width - x.shape[-1] - cols.shape[-1],), x.dtype)
    return jnp.concatenate([x, cols, pad], axis=-1)


def _heads_major(t, heads):
    b, s, _ = t.shape
    return t.reshape(b, s, heads, HEAD_DIM).transpose(0, 2, 1, 3)


def _overlapping_blocks(t):
    b, s, _ = t.shape
    chunks = _heads_major(t, NSA_KV_GROUPS).reshape(b, NSA_KV_GROUPS, s // CMP_STRIDE, CMP_STRIDE * HEAD_DIM)
    nxt = jnp.concatenate([chunks[:, :, 1:], jnp.zeros_like(chunks[:, :, :1])], axis=2)
    return jnp.concatenate([chunks, nxt], axis=-1)


def _split_w_in(w_in_l, b_in_l, d):
    sizes = (d, d, d, 512, 512, 512, FOX_HEADS, 512, 512, 512, 512, 128, 128, 128, 128, 128, 128, NSA_HEADS * 3)
    offs = np.concatenate([[0], np.cumsum(sizes)])
    col = lambda a, k: a[..., int(offs[k]):int(offs[k + 1])]
    (GA, GB, GC, FQ, FK, FV, FF, CB, CC, CX, NQ, NKC, NVC, NKS, NVS, NKW, NVW, NG) = range(18)
    pad = 384 - (128 + 128 + FOX_HEADS + NSA_HEADS * 3)

    prescale = {FQ: SCALE, NQ: SCALE}

    def group(idx, extra=0):
        w = jnp.concatenate([col(w_in_l, k) * prescale.get(k, 1.0) for k in idx], axis=-1)
        bias = jnp.concatenate([col(b_in_l, k) * prescale.get(k, 1.0) for k in idx], axis=-1)
        if extra:
            w = jnp.pad(w, ((0, 0), (0, extra)))
            bias = jnp.pad(bias, ((0, extra),))
        return w.astype(BF16), bias

    return (group((FQ, FK, FV, NQ, NKS, NVS, NKW, NVW)),
            group((GA, GB, GC, CB, CC, CX)),
            group((NKC, NVC, FF, NG), pad))


def kernel(x, norm_mix_g, w_in, b_in, cmp_pe_k, cmp_w1_k, cmp_w2_k, cmp_pe_v, cmp_w1_v, cmp_w2_v, sc_conv_w, w_br_a, w_br_b, w_br_c, w_o, norm_ffn_g, w_up, ffn_conv_w, w_down, norm_final_g):
    b, s, d = x.shape
    depth = w_in.shape[0]
    n = b * s
    nqb = s // Q_BLOCK
    n_cmp_pad = s // CMP_STRIDE
    selmap = jnp.asarray(_selection_map(s // SEL_LEN, n_cmp_pad), dtype=BF16)
    one_col = np.ones((1,), np.float32)
    q_cols = _alibi_query_cols(nqb)
    cmp_cols = _alibi_key_cols(np.arange(n_cmp_pad) * CMP_STRIDE + CMP_LEN - 1)
    key_cols = _alibi_key_cols(np.arange(s))
    key_block = (np.arange(s)[:, None] // SEL_LEN == np.arange(LANES)[None, :]).astype(np.float32)
    sel_key_cols = np.concatenate([key_cols, np.zeros((s, LANES - HEAD_DIM - 4), np.float32), key_block], axis=-1)

    xr = x.reshape(n, d)
    h = rmsnorm_rows(xr, norm_mix_g[0], BF16)
    for l in range(depth):
        (w_b16, b_b16), (w_f32, b_f32), (w_sm, b_sm) = _split_w_in(w_in[l], b_in[l], d)
        zb = matmul_bias(h, w_b16, b_b16, BF16, tn=512)
        zf = matmul_bias(h, w_f32, b_f32, F32, tn=512)
        zs = matmul_bias(h, w_sm, b_sm, F32, tn=384)
        zb3 = zb.reshape(b, s, -1)
        zs3 = zs.reshape(b, s, -1)

        fq = _heads_major(zb3[..., 0:512], FOX_HEADS)
        fk = _heads_major(zb3[..., 512:1024], FOX_HEADS)
        fv = _heads_major(zb3[..., 1024:1536], FOX_HEADS)
        cum = logsig_cumsum(zs3[..., 256:256 + FOX_HEADS].transpose(0, 2, 1))
        cum = cum.transpose(0, 2, 3, 1).astype(BF16)
        ones = jnp.ones_like(cum[..., 0:3])
        a_out = fox_attention(_append_cols(fq, jnp.concatenate([cum[..., 0:3], ones], axis=-1), LANES),
                              _append_cols(fk, jnp.concatenate([ones, cum[..., 3:6]], axis=-1), LANES),
                              _append_cols(fv, one_col, LANES))
        a_out = a_out.transpose(0, 2, 1, 3).reshape(n, FOX_HEADS * HEAD_DIM)

        vkc = compress_blocks(_overlapping_blocks(zs3[..., 0:128]), _overlapping_blocks(zs3[..., 128:256]),
                              cmp_pe_k[l].reshape(1, -1), cmp_pe_v[l].reshape(1, -1),
                              cmp_w1_k[l].astype(BF16), cmp_w2_k[l].astype(BF16),
                              cmp_w1_v[l].astype(BF16), cmp_w2_v[l].astype(BF16))
        nq = zb3[..., 1536:2048].reshape(b, nqb, Q_BLOCK, NSA_KV_GROUPS, NSA_HG, HEAD_DIM)
        nq = nq.transpose(0, 3, 1, 4, 2, 5).reshape(b, NSA_KV_GROUPS, nqb, NSA_ROWS, HEAD_DIM)
        ng = zs3[..., 264:264 + NSA_HEADS * 3].reshape(b, nqb, Q_BLOCK, NSA_KV_GROUPS, NSA_HG, 3)
        ng = ng.transpose(0, 3, 1, 4, 2, 5).reshape(b, NSA_KV_GROUPS, nqb, NSA_ROWS, 3)
        group_keys = lambda lo: _heads_major(zb3[..., lo:lo + NSA_KV_GROUPS * HEAD_DIM], NSA_KV_GROUPS)
        c_out = nsa_attention(
            _append_cols(nq, q_cols, LANES),
            _append_cols(vkc[..., HEAD_DIM:], cmp_cols, LANES), _append_cols(vkc[..., :HEAD_DIM], one_col, LANES),
            _append_cols(group_keys(2048), sel_key_cols, 2 * LANES), _append_cols(group_keys(2176), one_col, LANES),
            _append_cols(group_keys(2304), key_cols, LANES), _append_cols(group_keys(2432), one_col, LANES),
            ng, selmap)
        c_out = c_out.reshape(b, NSA_KV_GROUPS, nqb, NSA_HG, Q_BLOCK, HEAD_DIM)
        c_out = c_out.transpose(0, 2, 4, 1, 3, 5).reshape(n, NSA_HEADS * HEAD_DIM)

        mix = merge_branches(zf, a_out, c_out, w_br_a[l].astype(BF16), w_br_b[l].astype(BF16),
                             w_br_c[l].astype(BF16), sc_conv_w[l], s)
        xr, h2 = residual_matmul_norm(mix, w_o[l].astype(BF16), xr, norm_ffn_g[l], BF16)

        act = ffn_up(h2, w_up[l].astype(BF16), ffn_conv_w[l], s)
        if l + 1 < depth:
            xr, h = residual_matmul_norm(act, w_down[l].astype(BF16), xr, norm_mix_g[l + 1], BF16)
        else:
            out = residual_matmul_norm(act, w_down[l].astype(BF16), xr, norm_final_g, F32,
                                       keep_residual=False)
    return out.reshape(b, s, d)
```

```python
import functools

import numpy as np
import jax
import jax.numpy as jnp
from jax import lax
from jax.experimental import pallas as pl
from jax.experimental.pallas import tpu as pltpu

F32 = jnp.float32
BF16 = jnp.bfloat16

HEAD_DIM = 64
FOX_HEADS = 8
NSA_HEADS = 8
NSA_KV_GROUPS = 2
NSA_HG = NSA_HEADS // NSA_KV_GROUPS
CMP_LEN = 32
CMP_STRIDE = 16
SEL_LEN = 64
SEL_TOP = 16
WIN = 512
Q_BLOCK = 128
EPS = 1e-6
NEG_INF = -1e30
FORCE_BONUS = 1e4
SCALE = HEAD_DIM ** -0.5

LANES = 128
SUBLANES = 8
VMEM_LIMIT = 48 * 1024 * 1024
PROJ_ROWS = 1024

NT_DIMS = (((1,), (1,)), ((), ()))


def _cparams(n_parallel, n_arbitrary=0):
    return pltpu.CompilerParams(
        dimension_semantics=("parallel",) * n_parallel + ("arbitrary",) * n_arbitrary,
        vmem_limit_bytes=VMEM_LIMIT)


def _sigmoid(x):
    return 1.0 / (1.0 + jnp.exp(-x))


def _rms(x, g):
    ms = jnp.mean(x * x, axis=-1, keepdims=True)
    return x * lax.rsqrt(ms + EPS) * g


def _widen(x, width):
    return jnp.concatenate([x, jnp.zeros((x.shape[0], width - x.shape[1]), x.dtype)], axis=1).astype(BF16)


def _norm_kernel(x_ref, g_ref, o_ref):
    o_ref[...] = _rms(x_ref[...], g_ref[...]).astype(o_ref.dtype)


def rmsnorm_rows(x, g, out_dtype, tm=1024):
    n, d = x.shape
    tm = min(tm, n)
    return pl.pallas_call(
        _norm_kernel,
        out_shape=jax.ShapeDtypeStruct((n, d), out_dtype),
        grid=(n // tm,),
        in_specs=[pl.BlockSpec((tm, d), lambda i: (i, 0)),
                  pl.BlockSpec((1, d), lambda i: (0, 0))],
        out_specs=pl.BlockSpec((tm, d), lambda i: (i, 0)),
        compiler_params=_cparams(1),
        name="rmsnorm",
    )(x, g.reshape(1, d))


def _mm_bias_kernel(a_ref, w_ref, b_ref, o_ref):
    acc = jnp.dot(a_ref[...], w_ref[...], preferred_element_type=F32)
    o_ref[...] = (acc + b_ref[...]).astype(o_ref.dtype)


def matmul_bias(a, w, b, out_dtype, tn, tm=PROJ_ROWS):
    n, k = a.shape
    m = w.shape[1]
    tm = min(tm, n)
    return pl.pallas_call(
        _mm_bias_kernel,
        out_shape=jax.ShapeDtypeStruct((n, m), out_dtype),
        grid=(n // tm, m // tn),
        in_specs=[pl.BlockSpec((tm, k), lambda i, j: (i, 0)),
                  pl.BlockSpec((k, tn), lambda i, j: (0, j)),
                  pl.BlockSpec((1, tn), lambda i, j: (0, j))],
        out_specs=pl.BlockSpec((tm, tn), lambda i, j: (i, j)),
        compiler_params=_cparams(2),
        name="matmul_bias",
    )(a, w, b.reshape(1, m))


def _fox_qkv_kernel(a_ref, w_ref, b_ref, aug_ref, o_ref):
    which = pl.program_id(1)
    acc = jnp.dot(a_ref[...], w_ref[...], preferred_element_type=F32) + b_ref[...]
    lane = lax.broadcasted_iota(jnp.int32, (acc.shape[0], LANES), 1)
    ones_col = jnp.where(lane == HEAD_DIM, 1.0, 0.0).astype(BF16)
    for hh in range(FOX_HEADS):
        extra = jnp.where(which == 2, ones_col, aug_ref[hh])
        o_ref[hh] = _widen(acc[:, hh * HEAD_DIM:(hh + 1) * HEAD_DIM], LANES) + extra


def fox_qkv_projection(h, w3, b3, aug, batch, seq):
    n, d = h.shape
    tm = min(PROJ_ROWS, seq)
    tpb = seq // tm
    width = FOX_HEADS * HEAD_DIM
    return pl.pallas_call(
        _fox_qkv_kernel,
        out_shape=jax.ShapeDtypeStruct((3, batch, FOX_HEADS, seq, LANES), BF16),
        grid=(n // tm, 3),
        in_specs=[pl.BlockSpec((tm, d), lambda i, j: (i, 0)),
                  pl.BlockSpec((None, d, width), lambda i, j: (j, 0, 0)),
                  pl.BlockSpec((None, 1, width), lambda i, j: (j, 0, 0)),
                  pl.BlockSpec((None, None, FOX_HEADS, tm, LANES),
                               lambda i, j: (jnp.minimum(j, 1), i // tpb, 0, i % tpb, 0))],
        out_specs=pl.BlockSpec((None, None, FOX_HEADS, tm, LANES), lambda i, j: (j, i // tpb, 0, i % tpb, 0)),
        compiler_params=_cparams(2), name="fox_qkv_projection",
    )(h, w3, b3, aug)


def _nsa_q_kernel(a_ref, w_ref, b_ref, cols_ref, o_ref):
    acc = jnp.dot(a_ref[...], w_ref[...], preferred_element_type=F32) + b_ref[...]
    for g in range(NSA_KV_GROUPS):
        for hg in range(NSA_HG):
            c0 = (g * NSA_HG + hg) * HEAD_DIM
            for qb in range(acc.shape[0] // Q_BLOCK):
                rows = slice(hg * Q_BLOCK, (hg + 1) * Q_BLOCK)
                o_ref[g, qb, rows, :] = (_widen(acc[qb * Q_BLOCK:(qb + 1) * Q_BLOCK, c0:c0 + HEAD_DIM], LANES)
                                         + cols_ref[g, qb, rows, :])


def nsa_q_projection(h, w, b, q_cols, batch, seq):
    n, d = h.shape
    tm = min(PROJ_ROWS, seq)
    tpb = seq // tm
    nb = tm // Q_BLOCK
    rows = NSA_HG * Q_BLOCK
    return pl.pallas_call(
        _nsa_q_kernel,
        out_shape=jax.ShapeDtypeStruct((batch, NSA_KV_GROUPS, seq // Q_BLOCK, rows, LANES), BF16),
        grid=(n // tm,),
        in_specs=[pl.BlockSpec((tm, d), lambda i: (i, 0)),
                  pl.BlockSpec(w.shape, lambda i: (0, 0)),
                  pl.BlockSpec((1, w.shape[1]), lambda i: (0, 0)),
                  pl.BlockSpec((NSA_KV_GROUPS, nb, rows, LANES), lambda i: (0, i % tpb, 0, 0))],
        out_specs=pl.BlockSpec((None, NSA_KV_GROUPS, nb, rows, LANES), lambda i: (i // tpb, 0, i % tpb, 0, 0)),
        compiler_params=_cparams(1), name="nsa_q_projection",
    )(h, w, b.reshape(1, -1), q_cols)


def _nsa_kv_kernel(a_ref, w_ref, b_ref, selcols_ref, keycols_ref, onecol_ref, ks_ref, vs_ref, kw_ref, vw_ref):
    acc = jnp.dot(a_ref[...], w_ref[...], preferred_element_type=F32) + b_ref[...]
    gw = NSA_KV_GROUPS * HEAD_DIM
    for g in range(NSA_KV_GROUPS):
        part = lambda idx: acc[:, idx * gw + g * HEAD_DIM: idx * gw + (g + 1) * HEAD_DIM]
        ks_ref[g] = _widen(part(0), 2 * LANES) + selcols_ref[...]
        vs_ref[g] = _widen(part(1), LANES) + onecol_ref[...]
        kw_ref[g] = _widen(part(2), LANES) + keycols_ref[...]
        vw_ref[g] = _widen(part(3), LANES) + onecol_ref[...]


def nsa_kv_projection(h, w, b, sel_cols, key_cols, one_col, batch, seq):
    n, d = h.shape
    tm = min(PROJ_ROWS, seq)
    tpb = seq // tm
    out = lambda width: jax.ShapeDtypeStruct((batch, NSA_KV_GROUPS, seq, width), BF16)
    out_spec = lambda width: pl.BlockSpec((None, NSA_KV_GROUPS, tm, width), lambda i: (i // tpb, 0, i % tpb, 0))
    return pl.pallas_call(
        _nsa_kv_kernel,
        out_shape=(out(2 * LANES), out(LANES), out(LANES), out(LANES)),
        grid=(n // tm,),
        in_specs=[pl.BlockSpec((tm, d), lambda i: (i, 0)),
                  pl.BlockSpec(w.shape, lambda i: (0, 0)),
                  pl.BlockSpec((1, w.shape[1]), lambda i: (0, 0)),
                  pl.BlockSpec((tm, 2 * LANES), lambda i: (i % tpb, 0)),
                  pl.BlockSpec((tm, LANES), lambda i: (i % tpb, 0)),
                  pl.BlockSpec((1, LANES), lambda i: (0, 0))],
        out_specs=(out_spec(2 * LANES), out_spec(LANES), out_spec(LANES), out_spec(LANES)),
        compiler_params=_cparams(1), name="nsa_kv_projection",
    )(h, w, b.reshape(1, -1), sel_cols, key_cols, one_col)


def _res_norm_kernel(a_ref, w_ref, x_ref, g_ref, xo_ref, ho_ref):
    xn = x_ref[...] + jnp.dot(a_ref[...], w_ref[...], preferred_element_type=F32)
    xo_ref[...] = xn
    ho_ref[...] = _rms(xn, g_ref[...]).astype(ho_ref.dtype)


def _res_norm_only_kernel(a_ref, w_ref, x_ref, g_ref, ho_ref):
    xn = x_ref[...] + jnp.dot(a_ref[...], w_ref[...], preferred_element_type=F32)
    ho_ref[...] = _rms(xn, g_ref[...]).astype(ho_ref.dtype)


def residual_matmul_norm(a, w, x, g, norm_dtype, keep_residual=True, tm=512):
    n, k = a.shape
    d = w.shape[1]
    tm = min(tm, n)
    row = lambda i: (i, 0)
    in_specs = [pl.BlockSpec((tm, k), row),
                pl.BlockSpec((k, d), lambda i: (0, 0)),
                pl.BlockSpec((tm, d), row),
                pl.BlockSpec((1, d), lambda i: (0, 0))]
    if keep_residual:
        return pl.pallas_call(
            _res_norm_kernel,
            out_shape=(jax.ShapeDtypeStruct((n, d), F32), jax.ShapeDtypeStruct((n, d), norm_dtype)),
            grid=(n // tm,), in_specs=in_specs,
            out_specs=(pl.BlockSpec((tm, d), row), pl.BlockSpec((tm, d), row)),
            compiler_params=_cparams(1), name="residual_matmul_norm",
        )(a, w, x, g.reshape(1, d))
    return pl.pallas_call(
        _res_norm_only_kernel,
        out_shape=jax.ShapeDtypeStruct((n, d), norm_dtype),
        grid=(n // tm,), in_specs=in_specs,
        out_specs=pl.BlockSpec((tm, d), row),
        compiler_params=_cparams(1), name="residual_matmul_final_norm",
    )(a, w, x, g.reshape(1, d))


CUM_BLOCK = 256


def _bf16_pieces(x):
    hi = x.astype(BF16).astype(F32)
    r1 = x - hi
    mid = r1.astype(BF16).astype(F32)
    lo = (r1 - mid).astype(BF16).astype(F32)
    return hi, mid, lo


def _fox_bias_columns_kernel(f_ref, o_ref):
    s = f_ref.shape[1]
    x = jnp.broadcast_to(f_ref[...], (SUBLANES, s))
    ls = jnp.minimum(x, 0.0) - jnp.log1p(jnp.exp(-jnp.abs(x)))
    r = lax.broadcasted_iota(jnp.int32, (CUM_BLOCK, CUM_BLOCK), 0)
    c = lax.broadcasted_iota(jnp.int32, (CUM_BLOCK, CUM_BLOCK), 1)
    tri = jnp.where(r <= c, 1.0, 0.0).astype(BF16)
    eye = jnp.where(r == c, 1.0, 0.0).astype(BF16)
    lane_row = lax.broadcasted_iota(jnp.int32, (LANES, CUM_BLOCK), 0) - HEAD_DIM
    carry = jnp.zeros((SUBLANES, 1), F32)
    for blk in range(s // CUM_BLOCK):
        cols = slice(blk * CUM_BLOCK, (blk + 1) * CUM_BLOCK)
        cs = carry
        for piece in _bf16_pieces(ls[:, cols]):
            cs = cs + jnp.dot(piece.astype(BF16), tri, preferred_element_type=F32)
        carry = cs[:, CUM_BLOCK - 1:CUM_BLOCK]
        pieces = [p[0:1] for p in _bf16_pieces(cs)]
        q_side = jnp.where((lane_row >= 3) & (lane_row < 6), 1.0, 0.0)
        k_side = jnp.where((lane_row >= 0) & (lane_row < 3), 1.0, 0.0)
        for k, piece in enumerate(pieces):
            q_side = jnp.where(lane_row == k, piece, q_side)
            k_side = jnp.where(lane_row == 3 + k, -piece, k_side)
        for side, plane in ((q_side, 0), (k_side, 1)):
            o_ref[plane, pl.ds(blk * CUM_BLOCK, CUM_BLOCK), :] = lax.dot_general(
                eye, side.astype(BF16), NT_DIMS, preferred_element_type=F32).astype(o_ref.dtype)


def fox_bias_columns(f_t):
    b, h, _, s = f_t.shape
    return pl.pallas_call(
        _fox_bias_columns_kernel,
        out_shape=jax.ShapeDtypeStruct((2, b, h, s, LANES), BF16),
        grid=(b, h),
        in_specs=[pl.BlockSpec((None, None, 1, s), lambda i, j: (i, j, 0, 0))],
        out_specs=pl.BlockSpec((2, None, None, s, LANES), lambda i, j: (0, i, j, 0, 0)),
        compiler_params=_cparams(2), name="fox_bias_columns",
    )(f_t)


KEY_TILE = 512
ATTN_ROWS = 512
SOFTMAX_ROWS = 64


def _online_softmax_tiles(n_full, logits, values, last_bias, s_sc, p_sc, m_sc, a_sc, acc_sc):
    rows_total, tile = p_sc.shape
    rep = tile // LANES
    m_sc[...] = jnp.full_like(m_sc, NEG_INF)
    acc_sc[...] = jnp.zeros_like(acc_sc)

    def update(slot, t, last):
        for c in range(rows_total // SOFTMAX_ROWS):
            rows = pl.ds(c * SOFTMAX_ROWS, SOFTMAX_ROWS)
            s = s_sc[slot, rows, :]
            if last:
                s = last_bias(s, c * SOFTMAX_ROWS)
            m_prev = m_sc[rows, :]
            m_new = jnp.maximum(m_prev, jnp.max(s, axis=-1, keepdims=True))
            a_sc[rows, :] = jnp.exp(m_prev - m_new)
            m_sc[rows, :] = m_new
            p_sc[rows, :] = jnp.exp(s - jnp.concatenate([m_new] * rep, axis=1)).astype(BF16)
        acc_sc[...] = a_sc[...] * acc_sc[...] + jnp.dot(p_sc[...], values(t), preferred_element_type=F32)

    s_sc[0] = logits(0)

    def two_tiles(u, carry):
        t = 2 * u
        s_sc[1] = logits(t + 1)
        update(0, t, False)
        s_sc[0] = logits(t + 2)
        update(1, t + 1, False)
        return carry

    lax.fori_loop(0, n_full // 2, two_tiles, 0)

    @pl.when(n_full % 2 == 1)
    def _():
        s_sc[1] = logits(n_full)
        update(0, n_full - 1, False)
        update(1, n_full, True)

    @pl.when(n_full % 2 == 0)
    def _():
        update(0, n_full, True)


def _softmax_scratch(rows, tile):
    return [pltpu.VMEM((2, rows, tile), F32), pltpu.VMEM((rows, tile), BF16),
            pltpu.VMEM((rows, LANES), F32), pltpu.VMEM((rows, LANES), F32), pltpu.VMEM((rows, LANES), F32)]


def _normalized(acc):
    return acc[:, :HEAD_DIM] / acc[:, HEAD_DIM:HEAD_DIM + 1]


def _fox_kernel(q_ref, k_ref, v_ref, o_ref, s_sc, p_sc, m_sc, a_sc, acc_sc):
    tile = KEY_TILE
    i = pl.program_id(2)
    q = q_ref[...]

    def logits(t):
        k0 = pl.multiple_of(t * tile, tile)
        return lax.dot_general(q, k_ref[pl.ds(k0, tile), :], NT_DIMS, preferred_element_type=F32)

    def values(t):
        return v_ref[pl.ds(pl.multiple_of(t * tile, tile), tile), :]

    def causal(s, first_row):
        qi = first_row + lax.broadcasted_iota(jnp.int32, s.shape, 0)
        ki = lax.broadcasted_iota(jnp.int32, s.shape, 1)
        return jnp.where(ki <= qi, s, NEG_INF)

    _online_softmax_tiles(i, logits, values, causal, s_sc, p_sc, m_sc, a_sc, acc_sc)
    o_ref[...] = _normalized(acc_sc[...]).astype(o_ref.dtype)


def fox_attention(qkv):
    _, b, h, s, _ = qkv.shape
    tile = KEY_TILE
    assert s % tile == 0 and ATTN_ROWS == tile
    whole = lambda plane: pl.BlockSpec((None, None, None, s, LANES), lambda b_, h_, i: (plane, b_, h_, 0, 0))
    return pl.pallas_call(
        _fox_kernel,
        out_shape=jax.ShapeDtypeStruct((b, h, s, HEAD_DIM), BF16),
        grid=(b, h, s // tile),
        in_specs=[pl.BlockSpec((None, None, None, tile, LANES), lambda b_, h_, i: (0, b_, h_, i, 0)),
                  whole(1), whole(2)],
        out_specs=pl.BlockSpec((None, None, tile, HEAD_DIM), lambda b_, h_, i: (b_, h_, i, 0)),
        scratch_shapes=_softmax_scratch(ATTN_ROWS, tile),
        compiler_params=_cparams(3), name="fox_attention",
    )(qkv, qkv, qkv)


def _gelu_tanh(x):
    return 0.5 * x * (1.0 + jnp.tanh(np.sqrt(2.0 / np.pi).astype(np.float32) * (x + 0.044715 * (x * x * x))))


def _compress_kernel(bk_ref, bv_ref, pek_ref, pev_ref, w1k_ref, w2k_ref, w1v_ref, w2v_ref,
                     kcols_ref, onecol_ref, ko_ref, vo_ref):
    def mlp(blk_ref, pe_ref, w1_ref, w2_ref):
        xin = (blk_ref[...] + pe_ref[...]).astype(BF16)
        hid = _gelu_tanh(jnp.dot(xin, w1_ref[...], preferred_element_type=F32))
        return jnp.dot(hid.astype(BF16), w2_ref[...], preferred_element_type=F32)

    ko_ref[...] = _widen(mlp(bk_ref, pek_ref, w1k_ref, w2k_ref), LANES) + kcols_ref[...]
    vo_ref[...] = _widen(mlp(bv_ref, pev_ref, w1v_ref, w2v_ref), LANES) + onecol_ref[...]


def compress_blocks(blk_k, blk_v, pe_k, pe_v, w1_k, w2_k, w1_v, w2_v, key_cols, one_col):
    b, g, nc, feat = blk_k.shape
    blk_spec = pl.BlockSpec((None, None, nc, feat), lambda i, j: (i, j, 0, 0))
    full = lambda shape: pl.BlockSpec(shape, lambda i, j: (0,) * len(shape))
    out = jax.ShapeDtypeStruct((b, g, nc, LANES), BF16)
    out_spec = pl.BlockSpec((None, None, nc, LANES), lambda i, j: (i, j, 0, 0))
    return pl.pallas_call(
        _compress_kernel,
        out_shape=(out, out),
        grid=(b, g),
        in_specs=[blk_spec, blk_spec, full((1, feat)), full((1, feat)),
                  full((feat, HEAD_DIM)), full((HEAD_DIM, HEAD_DIM)),
                  full((feat, HEAD_DIM)), full((HEAD_DIM, HEAD_DIM)),
                  full((nc, LANES)), full((1, LANES))],
        out_specs=(out_spec, out_spec),
        compiler_params=_cparams(2), name="nsa_compress",
    )(blk_k, blk_v, pe_k, pe_v, w1_k, w2_k, w1_v, w2_v, key_cols, one_col)


NSA_ROWS = NSA_HG * Q_BLOCK


def _select_blocks_t(score_ref, n_sel, n_top):
    groups = n_sel // SUBLANES
    grp = [score_ref[pl.ds(g * SUBLANES, SUBLANES), :] for g in range(groups)]
    blk = lax.broadcasted_iota(jnp.int32, (SUBLANES, LANES), 0)
    beaten_by = [jnp.zeros((SUBLANES, LANES), F32) for _ in range(groups)]
    for k in range(n_sel):
        sk = jnp.broadcast_to(score_ref[pl.ds(k, 1), :], (SUBLANES, LANES))
        for g in range(groups):
            if g * SUBLANES > k:
                wins = sk >= grp[g]
            elif g * SUBLANES + SUBLANES - 1 < k:
                wins = sk > grp[g]
            else:
                wins = (sk > grp[g]) | ((sk == grp[g]) & (blk + g * SUBLANES > k))
            beaten_by[g] = beaten_by[g] + jnp.where(wins, 1.0, 0.0)
    return jnp.concatenate([jnp.where(c < n_top, 1.0, 0.0) for c in beaten_by], axis=0)


def _nsa_kernel(q_ref, kc_ref, vc_ref, ks_ref, vs_ref, kw_ref, vw_ref, gl_ref, selmap_ref, o_ref,
                score_sc, qa_sc, s_sc, p_sc, m_sc, a_sc, acc_sc, *, seq, n_top):
    i = pl.program_id(2)
    q0 = i * Q_BLOCK
    n_cmp_pad = seq // CMP_STRIDE
    n_sel = seq // SEL_LEN
    q1 = q_ref[...]
    row = lax.broadcasted_iota(jnp.int32, (NSA_ROWS, 1), 0)
    qpos_col = q0 + (row & (Q_BLOCK - 1))

    cend = lax.broadcasted_iota(jnp.int32, (1, n_cmp_pad), 1) * CMP_STRIDE + (CMP_LEN - 1)
    valid_c = cend <= qpos_col
    lc = jnp.where(valid_c, lax.dot_general(q1, kc_ref[...], NT_DIMS, preferred_element_type=F32), NEG_INF)
    mc = jnp.max(lc, axis=-1, keepdims=True)
    pc = jnp.where(valid_c, jnp.exp(lc - mc), 0.0)
    sc = jnp.sum(pc, axis=-1, keepdims=True)
    pc = pc / jnp.where(sc > 0, sc, 1.0)
    o_c = jnp.dot(pc.astype(BF16), vc_ref[...], preferred_element_type=F32)[:, :HEAD_DIM]

    band = WIN + Q_BLOCK
    w0 = pl.multiple_of(jnp.maximum(q0 - WIN, 0), Q_BLOCK)
    rel_w = (w0 + lax.broadcasted_iota(jnp.int32, (Q_BLOCK, band), 1)
             - (q0 + lax.broadcasted_iota(jnp.int32, (Q_BLOCK, band), 0)))
    bias_w = jnp.where((rel_w <= 0) & (rel_w > -WIN), 0.0, NEG_INF)
    sw = lax.dot_general(q1, kw_ref[pl.ds(w0, band), :], NT_DIMS, preferred_element_type=F32)
    sw = (sw.reshape(NSA_HG, Q_BLOCK, band) + bias_w[None]).reshape(NSA_ROWS, band)
    pw = jnp.exp(sw - jnp.max(sw, axis=-1, keepdims=True))
    o_w = _normalized(jnp.dot(pw.astype(BF16), vw_ref[pl.ds(w0, band), :], preferred_element_type=F32))
    gate = _sigmoid(gl_ref[...])
    out_cw = gate[:, 0:1] * o_c + gate[:, 2:3] * o_w

    pcs = pc[0:Q_BLOCK]
    for hh in range(1, NSA_HG):
        pcs = pcs + pc[hh * Q_BLOCK:(hh + 1) * Q_BLOCK]
    hi = pcs.astype(BF16)
    lo = (pcs - hi.astype(F32)).astype(BF16)
    selmap = selmap_ref[...]
    imp_t = (lax.dot_general(selmap, hi, NT_DIMS, preferred_element_type=F32)
             + lax.dot_general(selmap, lo, NT_DIMS, preferred_element_type=F32))
    blk = lax.broadcasted_iota(jnp.int32, (n_sel, Q_BLOCK), 0)
    qrow = q0 + lax.broadcasted_iota(jnp.int32, (n_sel, Q_BLOCK), 1)
    qblk = qrow >> 6
    forced = (blk == 0) | (blk == qblk) | (blk == qblk - 1)
    score_sc[...] = jnp.where(blk * SEL_LEN <= qrow, imp_t + jnp.where(forced, FORCE_BONUS, 0.0), NEG_INF)
    sel_t = _select_blocks_t(score_sc, n_sel, n_top)

    sel_t = jnp.concatenate([sel_t, jnp.ones((LANES - n_sel, Q_BLOCK), F32)], axis=0).astype(BF16)
    eye = jnp.where(lax.broadcasted_iota(jnp.int32, (Q_BLOCK, Q_BLOCK), 0)
                    == lax.broadcasted_iota(jnp.int32, (Q_BLOCK, Q_BLOCK), 1), 1.0, 0.0).astype(BF16)
    sel = lax.dot_general(eye, sel_t, NT_DIMS, preferred_element_type=F32)
    closed = jnp.where(sel > 0.5, 0.0, NEG_INF).astype(BF16)
    qa_sc[:, 0:LANES] = q1
    for hh in range(NSA_HG):
        qa_sc[hh * Q_BLOCK:(hh + 1) * Q_BLOCK, LANES:2 * LANES] = closed

    tile = KEY_TILE
    n_full = q0 // tile

    def logits(t):
        k0 = pl.multiple_of(t * tile, tile)
        return lax.dot_general(qa_sc[...], ks_ref[pl.ds(k0, tile), :], NT_DIMS, preferred_element_type=F32)

    def values(t):
        return vs_ref[pl.ds(pl.multiple_of(t * tile, tile), tile), :]

    def causal(s, first_row):
        r = first_row + lax.broadcasted_iota(jnp.int32, s.shape, 0)
        kpos = n_full * tile + lax.broadcasted_iota(jnp.int32, s.shape, 1)
        return jnp.where(kpos <= q0 + (r & (Q_BLOCK - 1)), s, NEG_INF)

    _online_softmax_tiles(n_full, logits, values, causal, s_sc, p_sc, m_sc, a_sc, acc_sc)
    o_s = _normalized(acc_sc[...])
    o_ref[...] = (out_cw + gate[:, 1:2] * o_s).astype(o_ref.dtype)


def nsa_attention(q_aug, kc_aug, vc_aug, ks_aug, vs_aug, kw_aug, vw_aug, gate_logit, selmap):
    b, g, nb = q_aug.shape[:3]
    s = vs_aug.shape[2]
    nc = kc_aug.shape[2]
    n_sel = s // SEL_LEN
    assert s % KEY_TILE == 0 and s >= WIN + Q_BLOCK and n_sel <= LANES and NSA_ROWS == ATTN_ROWS
    per_block = lambda width: pl.BlockSpec((None, None, None, NSA_ROWS, width),
                                           lambda b_, g_, i: (b_, g_, i, 0, 0))
    per_group = lambda rows, width: pl.BlockSpec((None, None, rows, width), lambda b_, g_, i: (b_, g_, 0, 0))
    return pl.pallas_call(
        functools.partial(_nsa_kernel, seq=s, n_top=min(SEL_TOP, n_sel)),
        out_shape=jax.ShapeDtypeStruct((b, g, nb, NSA_ROWS, HEAD_DIM), BF16),
        grid=(b, g, nb),
        in_specs=[per_block(LANES), per_group(nc, LANES), per_group(nc, LANES),
                  per_group(s, 2 * LANES), per_group(s, LANES), per_group(s, LANES), per_group(s, LANES),
                  per_block(3), pl.BlockSpec((n_sel, nc), lambda b_, g_, i: (0, 0))],
        out_specs=per_block(HEAD_DIM),
        scratch_shapes=[pltpu.VMEM((n_sel, Q_BLOCK), F32), pltpu.VMEM((NSA_ROWS, 2 * LANES), BF16)]
        + _softmax_scratch(NSA_ROWS, KEY_TILE),
        compiler_params=_cparams(3), name="nsa_attention",
    )(q_aug, kc_aug, vc_aug, ks_aug, vs_aug, kw_aug, vw_aug, gate_logit, selmap)


def _causal_conv3(u, halo, w):
    row = lax.broadcasted_iota(jnp.int32, u.shape, 0)
    last = halo.shape[0] - 1
    prev1 = halo[last:last + 1]
    prev2 = halo[last - 1:last]
    u1 = jnp.where(row == 0, prev1, pltpu.roll(u, 1, 0))
    u2 = jnp.where(row == 0, prev2, jnp.where(row == 1, prev1, pltpu.roll(u, 2, 0)))
    return w[0:1] * u2 + w[1:2] * u1 + w[2:3] * u


def _merge_kernel(ga_ref, gb_ref, gc_ref, cb_ref, cc_ref, cx_ref, cch_ref, cxh_ref, a_ref, c_ref,
                  wa_ref, wb_ref, wc_ref, cw_ref, o_ref, a_sc, c_sc, *, tm, seq):
    for hh in range(FOX_HEADS):
        a_sc[:, hh * HEAD_DIM:(hh + 1) * HEAD_DIM] = a_ref[hh]
    for g in range(NSA_KV_GROUPS):
        for hg in range(NSA_HG):
            c0 = (g * NSA_HG + hg) * HEAD_DIM
            for qb in range(tm // Q_BLOCK):
                c_sc[qb * Q_BLOCK:(qb + 1) * Q_BLOCK, c0:c0 + HEAD_DIM] = c_ref[g, qb, hg * Q_BLOCK:(hg + 1) * Q_BLOCK, :]
    seq_start = (pl.program_id(0) * tm) % seq == 0
    u = cc_ref[...] * cx_ref[...]
    halo = jnp.where(seq_start, 0.0, cch_ref[...] * cxh_ref[...])
    conv = _causal_conv3(u, halo, cw_ref[...])
    y_a = jnp.dot(a_sc[...], wa_ref[...], preferred_element_type=F32)
    y_b = jnp.dot((cb_ref[...] * conv).astype(BF16), wb_ref[...], preferred_element_type=F32)
    y_c = jnp.dot(c_sc[...], wc_ref[...], preferred_element_type=F32)
    mix = _sigmoid(ga_ref[...]) * y_a + _sigmoid(gb_ref[...]) * y_b + _sigmoid(gc_ref[...]) * y_c
    o_ref[...] = mix.astype(o_ref.dtype)


def merge_branches(zf, a_out, c_out, w_a, w_b, w_c, conv_w, seq, tm=256):
    n = zf.shape[0]
    cw, d = w_b.shape
    tm = min(tm, n)
    tpb = seq // tm
    nb = tm // Q_BLOCK
    gcol = lambda c: pl.BlockSpec((tm, d), lambda i: (i, c))
    ccol = lambda c: pl.BlockSpec((tm, cw), lambda i: (i, 3 * d // cw + c))
    hcol = lambda c: pl.BlockSpec((SUBLANES, cw),
                                  lambda i: (jnp.maximum(i * (tm // SUBLANES) - 1, 0), 3 * d // cw + c))
    full = lambda shape: pl.BlockSpec(shape, lambda i: (0, 0))
    return pl.pallas_call(
        functools.partial(_merge_kernel, tm=tm, seq=seq),
        out_shape=jax.ShapeDtypeStruct((n, d), BF16),
        grid=(n // tm,),
        in_specs=[gcol(0), gcol(1), gcol(2), ccol(0), ccol(1), ccol(2), hcol(1), hcol(2),
                  pl.BlockSpec((None, FOX_HEADS, tm, HEAD_DIM), lambda i: (i // tpb, 0, i % tpb, 0)),
                  pl.BlockSpec((None, NSA_KV_GROUPS, nb, NSA_ROWS, HEAD_DIM), lambda i: (i // tpb, 0, i % tpb, 0, 0)),
                  full(w_a.shape), full(w_b.shape), full(w_c.shape), full(conv_w.shape)],
        out_specs=pl.BlockSpec((tm, d), lambda i: (i, 0)),
        scratch_shapes=[pltpu.VMEM((tm, FOX_HEADS * HEAD_DIM), BF16), pltpu.VMEM((tm, NSA_HEADS * HEAD_DIM), BF16)],
        compiler_params=_cparams(1), name="merge_branches",
    )(zf, zf, zf, zf, zf, zf, zf, zf, a_out, c_out, w_a, w_b, w_c, conv_w)


def _ffn_up_kernel(h_ref, hh_ref, wu_ref, wv_ref, cw_ref, o_ref, *, tm, seq):
    seq_start = (pl.program_id(1) * tm) % seq == 0
    wu = wu_ref[...]
    u = jnp.dot(h_ref[...], wu, preferred_element_type=F32)
    v = jnp.dot(h_ref[...], wv_ref[...], preferred_element_type=F32)
    halo = jnp.where(seq_start, 0.0, jnp.dot(hh_ref[...], wu, preferred_element_type=F32))
    c = _causal_conv3(u, halo, cw_ref[...])
    o_ref[...] = (c * _sigmoid(c) * v).astype(o_ref.dtype)


def ffn_up(h, w_up, conv_w, seq, tm=512):
    n, d = h.shape
    f = conv_w.shape[1]
    tm = min(tm, n)
    nj = 2 if (f // 2) % LANES == 0 else 1
    tn = f // nj
    return pl.pallas_call(
        functools.partial(_ffn_up_kernel, tm=tm, seq=seq),
        out_shape=jax.ShapeDtypeStruct((n, f), BF16),
        grid=(nj, n // tm),
        in_specs=[pl.BlockSpec((tm, d), lambda j, i: (i, 0)),
                  pl.BlockSpec((2 * SUBLANES, d), lambda j, i: (jnp.maximum(i * (tm // (2 * SUBLANES)) - 1, 0), 0)),
                  pl.BlockSpec((d, tn), lambda j, i: (0, j)),
                  pl.BlockSpec((d, tn), lambda j, i: (0, nj + j)),
                  pl.BlockSpec((conv_w.shape[0], tn), lambda j, i: (0, j))],
        out_specs=pl.BlockSpec((tm, tn), lambda j, i: (i, j)),
        compiler_params=_cparams(2), name="ffn_up",
    )(h, h, w_up, w_up, conv_w)


def _selection_map(n_sel, n_cmp_pad):
    sel_start = np.arange(n_sel)[:, None] * SEL_LEN
    cmp_start = np.arange(n_cmp_pad)[None, :] * CMP_STRIDE
    ov = np.minimum(sel_start + SEL_LEN, cmp_start + CMP_LEN) - np.maximum(sel_start, cmp_start)
    ov = np.clip(ov, 0, None) / CMP_STRIDE
    ov[:, n_cmp_pad - 1] = 0.0
    return ov.astype(np.float32)


def _bf16_exact(a):
    a = np.asarray(a, np.float32)
    assert np.array_equal(a, a.astype(BF16).astype(np.float32)), "constant is not exact in bf16"
    return a


def _after_head(cols, width=LANES):
    out = np.zeros(cols.shape[:-1] + (width,), np.float32)
    out[..., HEAD_DIM:HEAD_DIM + cols.shape[-1]] = cols
    return out


def _alibi_query_cols(nqb):
    slope = np.array([2.0 ** (-8.0 * (i + 1) / NSA_HEADS) for i in range(NSA_HEADS)], np.float32)
    slope = slope.reshape(NSA_KV_GROUPS, 1, NSA_HG, 1)
    qpos = (np.arange(nqb)[:, None] * Q_BLOCK + np.arange(Q_BLOCK)[None, :]).reshape(1, nqb, 1, Q_BLOCK)
    cols = np.stack(np.broadcast_arrays(slope * SEL_LEN, slope, -slope * SEL_LEN * (qpos // SEL_LEN),
                                        -slope * (qpos % SEL_LEN)), axis=-1)
    return _bf16_exact(cols.reshape(NSA_KV_GROUPS, nqb, NSA_ROWS, 4))


def _alibi_key_cols(pos):
    pos = np.asarray(pos)
    return _bf16_exact(np.stack([pos // SEL_LEN, pos % SEL_LEN, np.ones_like(pos), np.ones_like(pos)], axis=-1))


def _heads_major(t, heads):
    b, s, _ = t.shape
    return t.reshape(b, s, heads, HEAD_DIM).transpose(0, 2, 1, 3)


def _overlapping_blocks(t):
    b, s, _ = t.shape
    chunks = _heads_major(t, NSA_KV_GROUPS).reshape(b, NSA_KV_GROUPS, s // CMP_STRIDE, CMP_STRIDE * HEAD_DIM)
    nxt = jnp.concatenate([chunks[:, :, 1:], jnp.zeros_like(chunks[:, :, :1])], axis=2)
    return jnp.concatenate([chunks, nxt], axis=-1)


def _split_w_in(w_in_l, b_in_l, d):
    sizes = (d, d, d, 512, 512, 512, FOX_HEADS, 512, 512, 512, 512, 128, 128, 128, 128, 128, 128, NSA_HEADS * 3)
    offs = np.concatenate([[0], np.cumsum(sizes)])
    col = lambda a, k: a[..., int(offs[k]):int(offs[k + 1])]
    (GA, GB, GC, FQ, FK, FV, FF, CB, CC, CX, NQ, NKC, NVC, NKS, NVS, NKW, NVW, NG) = range(18)
    pad = 384 - (128 + 128 + FOX_HEADS + NSA_HEADS * 3)

    prescale = {FQ: SCALE, NQ: SCALE}

    def group(idx, extra=0):
        w = jnp.concatenate([col(w_in_l, k) * prescale.get(k, 1.0) for k in idx], axis=-1)
        bias = jnp.concatenate([col(b_in_l, k) * prescale.get(k, 1.0) for k in idx], axis=-1)
        if extra:
            w = jnp.pad(w, ((0, 0), (0, extra)))
            bias = jnp.pad(bias, ((0, extra),))
        return w.astype(BF16), bias

    fox = [group((k,)) for k in (FQ, FK, FV)]
    return ((jnp.stack([w for w, _ in fox]), jnp.stack([b_[None] for _, b_ in fox])),
            group((NQ,)), group((NKS, NVS, NKW, NVW)),
            group((GA, GB, GC, CB, CC, CX)), group((NKC, NVC, FF, NG), pad))


def kernel(x, norm_mix_g, w_in, b_in, cmp_pe_k, cmp_w1_k, cmp_w2_k, cmp_pe_v, cmp_w1_v, cmp_w2_v, sc_conv_w, w_br_a, w_br_b, w_br_c, w_o, norm_ffn_g, w_up, ffn_conv_w, w_down, norm_final_g):
    b, s, d = x.shape
    depth = w_in.shape[0]
    n = b * s
    nqb = s // Q_BLOCK
    n_cmp_pad = s // CMP_STRIDE
    const = lambda a: jnp.asarray(a, dtype=BF16)
    selmap = const(_selection_map(s // SEL_LEN, n_cmp_pad))
    one_col = const(_after_head(np.ones((1, 1), np.float32)))
    q_cols = const(_after_head(_alibi_query_cols(nqb)))
    cmp_cols = const(_after_head(_alibi_key_cols(np.arange(n_cmp_pad) * CMP_STRIDE + CMP_LEN - 1)))
    key_cols = _after_head(_alibi_key_cols(np.arange(s)))
    key_block = (np.arange(s)[:, None] // SEL_LEN == np.arange(LANES)[None, :]).astype(np.float32)
    sel_cols = const(np.concatenate([key_cols, key_block], axis=-1))
    key_cols = const(key_cols)

    xr = x.reshape(n, d)
    h = rmsnorm_rows(xr, norm_mix_g[0], BF16)
    for l in range(depth):
        (w_fox, b_fox), (w_nq, b_nq), (w_nkv, b_nkv), (w_f32, b_f32), (w_sm, b_sm) = _split_w_in(w_in[l], b_in[l], d)
        zf = matmul_bias(h, w_f32, b_f32, F32, tn=512)
        zs3 = matmul_bias(h, w_sm, b_sm, F32, tn=384).reshape(b, s, -1)

        bias_cols = fox_bias_columns(zs3[..., 256:256 + FOX_HEADS].transpose(0, 2, 1)[:, :, None, :])
        a_out = fox_attention(fox_qkv_projection(h, w_fox, b_fox, bias_cols, b, s))

        kc_aug, vc_aug = compress_blocks(
            _overlapping_blocks(zs3[..., 0:128]), _overlapping_blocks(zs3[..., 128:256]),
            cmp_pe_k[l].reshape(1, -1), cmp_pe_v[l].reshape(1, -1),
            cmp_w1_k[l].astype(BF16), cmp_w2_k[l].astype(BF16),
            cmp_w1_v[l].astype(BF16), cmp_w2_v[l].astype(BF16), cmp_cols, one_col)
        ng = zs3[..., 264:264 + NSA_HEADS * 3].reshape(b, nqb, Q_BLOCK, NSA_KV_GROUPS, NSA_HG, 3)
        ng = ng.transpose(0, 3, 1, 4, 2, 5).reshape(b, NSA_KV_GROUPS, nqb, NSA_ROWS, 3)
        ks_aug, vs_aug, kw_aug, vw_aug = nsa_kv_projection(h, w_nkv, b_nkv, sel_cols, key_cols, one_col, b, s)
        c_out = nsa_attention(nsa_q_projection(h, w_nq, b_nq, q_cols, b, s), kc_aug, vc_aug,
                              ks_aug, vs_aug, kw_aug, vw_aug, ng, selmap)

        mix = merge_branches(zf, a_out, c_out, w_br_a[l].astype(BF16), w_br_b[l].astype(BF16),
                             w_br_c[l].astype(BF16), sc_conv_w[l], s)
        xr, h2 = residual_matmul_norm(mix, w_o[l].astype(BF16), xr, norm_ffn_g[l], BF16)

        act = ffn_up(h2, w_up[l].astype(BF16), ffn_conv_w[l], s)
        if l + 1 < depth:
            xr, h = residual_matmul_norm(act, w_down[l].astype(BF16), xr, norm_mix_g[l + 1], BF16)
        else:
            out = residual_matmul_norm(act, w_down[l].astype(BF16), xr, norm_final_g, F32,
                                       keep_residual=False)
    return out.reshape(b, s, d)
```

```python
import functools

import numpy as np
import jax
import jax.numpy as jnp
from jax import lax
from jax.experimental import pallas as pl
from jax.experimental.pallas import tpu as pltpu

F32 = jnp.float32
BF16 = jnp.bfloat16

HEAD_DIM = 64
FOX_HEADS = 8
NSA_HEADS = 8
NSA_KV_GROUPS = 2
NSA_HG = NSA_HEADS // NSA_KV_GROUPS
CMP_LEN = 32
CMP_STRIDE = 16
SEL_LEN = 64
SEL_TOP = 16
WIN = 512
Q_BLOCK = 128
EPS = 1e-6
NEG_INF = -1e30
FORCE_BONUS = 1e4
SCALE = HEAD_DIM ** -0.5

LANES = 128
SUBLANES = 8
VMEM_LIMIT = 48 * 1024 * 1024
PROJ_ROWS = 1024

NT_DIMS = (((1,), (1,)), ((), ()))


def _cparams(n_parallel, n_arbitrary=0):
    return pltpu.CompilerParams(
        dimension_semantics=("parallel",) * n_parallel + ("arbitrary",) * n_arbitrary,
        vmem_limit_bytes=VMEM_LIMIT)


def _sigmoid(x):
    return 1.0 / (1.0 + jnp.exp(-x))


def _rms(x, g):
    ms = jnp.mean(x * x, axis=-1, keepdims=True)
    return x * lax.rsqrt(ms + EPS) * g


def _widen(x, width):
    return jnp.concatenate([x, jnp.zeros((x.shape[0], width - x.shape[1]), x.dtype)], axis=1).astype(BF16)


def _norm_kernel(x_ref, g_ref, o_ref):
    o_ref[...] = _rms(x_ref[...], g_ref[...]).astype(o_ref.dtype)


def rmsnorm_rows(x, g, out_dtype, tm=1024):
    n, d = x.shape
    tm = min(tm, n)
    return pl.pallas_call(
        _norm_kernel,
        out_shape=jax.ShapeDtypeStruct((n, d), out_dtype),
        grid=(n // tm,),
        in_specs=[pl.BlockSpec((tm, d), lambda i: (i, 0)),
                  pl.BlockSpec((1, d), lambda i: (0, 0))],
        out_specs=pl.BlockSpec((tm, d), lambda i: (i, 0)),
        compiler_params=_cparams(1),
        name="rmsnorm",
    )(x, g.reshape(1, d))


def _mm_bias_kernel(a_ref, w_ref, b_ref, o_ref):
    acc = jnp.dot(a_ref[...], w_ref[...], preferred_element_type=F32)
    o_ref[...] = (acc + b_ref[...]).astype(o_ref.dtype)


def matmul_bias(a, w, b, out_dtype, tn, tm=PROJ_ROWS):
    n, k = a.shape
    m = w.shape[1]
    tm = min(tm, n)
    return pl.pallas_call(
        _mm_bias_kernel,
        out_shape=jax.ShapeDtypeStruct((n, m), out_dtype),
        grid=(n // tm, m // tn),
        in_specs=[pl.BlockSpec((tm, k), lambda i, j: (i, 0)),
                  pl.BlockSpec((k, tn), lambda i, j: (0, j)),
                  pl.BlockSpec((1, tn), lambda i, j: (0, j))],
        out_specs=pl.BlockSpec((tm, tn), lambda i, j: (i, j)),
        compiler_params=_cparams(2),
        name="matmul_bias",
    )(a, w, b.reshape(1, m))


def _fox_qkv_kernel(a_ref, w_ref, b_ref, aug_ref, o_ref):
    which = pl.program_id(1)
    acc = jnp.dot(a_ref[...], w_ref[...], preferred_element_type=F32) + b_ref[...]
    lane = lax.broadcasted_iota(jnp.int32, (acc.shape[0], LANES), 1)
    ones_col = jnp.where(lane == HEAD_DIM, 1.0, 0.0).astype(BF16)
    for hh in range(FOX_HEADS):
        extra = jnp.where(which == 2, ones_col, aug_ref[hh])
        o_ref[hh] = _widen(acc[:, hh * HEAD_DIM:(hh + 1) * HEAD_DIM], LANES) + extra


def fox_qkv_projection(h, w3, b3, aug, batch, seq):
    n, d = h.shape
    tm = min(PROJ_ROWS, seq)
    tpb = seq // tm
    width = FOX_HEADS * HEAD_DIM
    return pl.pallas_call(
        _fox_qkv_kernel,
        out_shape=jax.ShapeDtypeStruct((3, batch, FOX_HEADS, seq, LANES), BF16),
        grid=(n // tm, 3),
        in_specs=[pl.BlockSpec((tm, d), lambda i, j: (i, 0)),
                  pl.BlockSpec((None, d, width), lambda i, j: (j, 0, 0)),
                  pl.BlockSpec((None, 1, width), lambda i, j: (j, 0, 0)),
                  pl.BlockSpec((None, None, FOX_HEADS, tm, LANES),
                               lambda i, j: (jnp.minimum(j, 1), i // tpb, 0, i % tpb, 0))],
        out_specs=pl.BlockSpec((None, None, FOX_HEADS, tm, LANES), lambda i, j: (j, i // tpb, 0, i % tpb, 0)),
        compiler_params=_cparams(2), name="fox_qkv_projection",
    )(h, w3, b3, aug)


def _nsa_q_kernel(a_ref, w_ref, b_ref, cols_ref, o_ref):
    acc = jnp.dot(a_ref[...], w_ref[...], preferred_element_type=F32) + b_ref[...]
    for g in range(NSA_KV_GROUPS):
        for hg in range(NSA_HG):
            c0 = (g * NSA_HG + hg) * HEAD_DIM
            for qb in range(acc.shape[0] // Q_BLOCK):
                rows = slice(hg * Q_BLOCK, (hg + 1) * Q_BLOCK)
                o_ref[g, qb, rows, :] = (_widen(acc[qb * Q_BLOCK:(qb + 1) * Q_BLOCK, c0:c0 + HEAD_DIM], LANES)
                                         + cols_ref[g, qb, rows, :])


def nsa_q_projection(h, w, b, q_cols, batch, seq):
    n, d = h.shape
    tm = min(PROJ_ROWS, seq)
    tpb = seq // tm
    nb = tm // Q_BLOCK
    rows = NSA_HG * Q_BLOCK
    return pl.pallas_call(
        _nsa_q_kernel,
        out_shape=jax.ShapeDtypeStruct((batch, NSA_KV_GROUPS, seq // Q_BLOCK, rows, LANES), BF16),
        grid=(n // tm,),
        in_specs=[pl.BlockSpec((tm, d), lambda i: (i, 0)),
                  pl.BlockSpec(w.shape, lambda i: (0, 0)),
                  pl.BlockSpec((1, w.shape[1]), lambda i: (0, 0)),
                  pl.BlockSpec((NSA_KV_GROUPS, nb, rows, LANES), lambda i: (0, i % tpb, 0, 0))],
        out_specs=pl.BlockSpec((None, NSA_KV_GROUPS, nb, rows, LANES), lambda i: (i // tpb, 0, i % tpb, 0, 0)),
        compiler_params=_cparams(1), name="nsa_q_projection",
    )(h, w, b.reshape(1, -1), q_cols)


def _nsa_kv_kernel(a_ref, w_ref, b_ref, selcols_ref, keycols_ref, onecol_ref, ks_ref, vs_ref, kw_ref, vw_ref):
    acc = jnp.dot(a_ref[...], w_ref[...], preferred_element_type=F32) + b_ref[...]
    gw = NSA_KV_GROUPS * HEAD_DIM
    for g in range(NSA_KV_GROUPS):
        part = lambda idx: acc[:, idx * gw + g * HEAD_DIM: idx * gw + (g + 1) * HEAD_DIM]
        ks_ref[g] = _widen(part(0), 2 * LANES) + selcols_ref[...]
        vs_ref[g] = _widen(part(1), LANES) + onecol_ref[...]
        kw_ref[g] = _widen(part(2), LANES) + keycols_ref[...]
        vw_ref[g] = _widen(part(3), LANES) + onecol_ref[...]


def nsa_kv_projection(h, w, b, sel_cols, key_cols, one_col, batch, seq):
    n, d = h.shape
    tm = min(PROJ_ROWS, seq)
    tpb = seq // tm
    out = lambda width: jax.ShapeDtypeStruct((batch, NSA_KV_GROUPS, seq, width), BF16)
    out_spec = lambda width: pl.BlockSpec((None, NSA_KV_GROUPS, tm, width), lambda i: (i // tpb, 0, i % tpb, 0))
    return pl.pallas_call(
        _nsa_kv_kernel,
        out_shape=(out(2 * LANES), out(LANES), out(LANES), out(LANES)),
        grid=(n // tm,),
        in_specs=[pl.BlockSpec((tm, d), lambda i: (i, 0)),
                  pl.BlockSpec(w.shape, lambda i: (0, 0)),
                  pl.BlockSpec((1, w.shape[1]), lambda i: (0, 0)),
                  pl.BlockSpec((tm, 2 * LANES), lambda i: (i % tpb, 0)),
                  pl.BlockSpec((tm, LANES), lambda i: (i % tpb, 0)),
                  pl.BlockSpec((1, LANES), lambda i: (0, 0))],
        out_specs=(out_spec(2 * LANES), out_spec(LANES), out_spec(LANES), out_spec(LANES)),
        compiler_params=_cparams(1), name="nsa_kv_projection",
    )(h, w, b.reshape(1, -1), sel_cols, key_cols, one_col)


def _res_norm_kernel(a_ref, w_ref, x_ref, g_ref, xo_ref, ho_ref):
    xn = x_ref[...] + jnp.dot(a_ref[...], w_ref[...], preferred_element_type=F32)
    xo_ref[...] = xn
    ho_ref[...] = _rms(xn, g_ref[...]).astype(ho_ref.dtype)


def _res_norm_only_kernel(a_ref, w_ref, x_ref, g_ref, ho_ref):
    xn = x_ref[...] + jnp.dot(a_ref[...], w_ref[...], preferred_element_type=F32)
    ho_ref[...] = _rms(xn, g_ref[...]).astype(ho_ref.dtype)


def residual_matmul_norm(a, w, x, g, norm_dtype, keep_residual=True, tm=512):
    n, k = a.shape
    d = w.shape[1]
    tm = min(tm, n)
    row = lambda i: (i, 0)
    in_specs = [pl.BlockSpec((tm, k), row),
                pl.BlockSpec((k, d), lambda i: (0, 0)),
                pl.BlockSpec((tm, d), row),
                pl.BlockSpec((1, d), lambda i: (0, 0))]
    if keep_residual:
        return pl.pallas_call(
            _res_norm_kernel,
            out_shape=(jax.ShapeDtypeStruct((n, d), F32), jax.ShapeDtypeStruct((n, d), norm_dtype)),
            grid=(n // tm,), in_specs=in_specs,
            out_specs=(pl.BlockSpec((tm, d), row), pl.BlockSpec((tm, d), row)),
            compiler_params=_cparams(1), name="residual_matmul_norm",
        )(a, w, x, g.reshape(1, d))
    return pl.pallas_call(
        _res_norm_only_kernel,
        out_shape=jax.ShapeDtypeStruct((n, d), norm_dtype),
        grid=(n // tm,), in_specs=in_specs,
        out_specs=pl.BlockSpec((tm, d), row),
        compiler_params=_cparams(1), name="residual_matmul_final_norm",
    )(a, w, x, g.reshape(1, d))


CUM_BLOCK = 256


def _bf16_pieces(x):
    hi = x.astype(BF16).astype(F32)
    r1 = x - hi
    mid = r1.astype(BF16).astype(F32)
    lo = (r1 - mid).astype(BF16).astype(F32)
    return hi, mid, lo


CUM_CHUNK = 1024


def _fox_bias_columns_kernel(f_ref, o_ref, carry_sc):
    heads, chunk = f_ref.shape

    @pl.when(pl.program_id(1) == 0)
    def _():
        carry_sc[...] = jnp.zeros_like(carry_sc)

    x = f_ref[...]
    ls = jnp.minimum(x, 0.0) - jnp.log1p(jnp.exp(-jnp.abs(x)))
    r = lax.broadcasted_iota(jnp.int32, (CUM_BLOCK, CUM_BLOCK), 0)
    c = lax.broadcasted_iota(jnp.int32, (CUM_BLOCK, CUM_BLOCK), 1)
    tri = jnp.where(r <= c, 1.0, 0.0).astype(BF16)
    eye = jnp.where(r == c, 1.0, 0.0).astype(BF16)
    lane_q = lax.broadcasted_iota(jnp.int32, (2 * LANES, CUM_BLOCK), 0) - HEAD_DIM
    lane_k = lane_q - LANES
    ones_rows = jnp.where(((lane_q >= 3) & (lane_q < 6)) | ((lane_k >= 0) & (lane_k < 3)), 1.0, 0.0)
    carry = carry_sc[...]
    for blk in range(chunk // CUM_BLOCK):
        cols = slice(blk * CUM_BLOCK, (blk + 1) * CUM_BLOCK)
        cs = carry
        for piece in _bf16_pieces(ls[:, cols]):
            cs = cs + jnp.dot(piece.astype(BF16), tri, preferred_element_type=F32)
        carry = cs[:, CUM_BLOCK - 1:CUM_BLOCK]
        pieces = _bf16_pieces(cs)
        for hh in range(heads):
            both = ones_rows
            for k, piece in enumerate(pieces):
                both = jnp.where(lane_q == k, piece[hh:hh + 1], both)
                both = jnp.where(lane_k == 3 + k, -piece[hh:hh + 1], both)
            cols_t = lax.dot_general(eye, both.astype(BF16), NT_DIMS, preferred_element_type=F32)
            o_ref[0, hh, cols, :] = cols_t[:, :LANES].astype(o_ref.dtype)
            o_ref[1, hh, cols, :] = cols_t[:, LANES:].astype(o_ref.dtype)
    carry_sc[...] = carry


def fox_bias_columns(f_t):
    b, h, s = f_t.shape
    chunk = min(CUM_CHUNK, s)
    return pl.pallas_call(
        _fox_bias_columns_kernel,
        out_shape=jax.ShapeDtypeStruct((2, b, h, s, LANES), BF16),
        grid=(b, s // chunk),
        in_specs=[pl.BlockSpec((None, h, chunk), lambda i, j: (i, 0, j))],
        out_specs=pl.BlockSpec((2, None, h, chunk, LANES), lambda i, j: (0, i, 0, j, 0)),
        scratch_shapes=[pltpu.VMEM((h, 1), F32)],
        compiler_params=_cparams(1, 1), name="fox_bias_columns",
    )(f_t)


KEY_TILE = 512
ATTN_ROWS = 512
SOFTMAX_ROWS = 64


def _online_softmax_tiles(n_full, logits, values, last_bias, s_sc, p_sc, m_sc, a_sc, acc_sc):
    rows_total, tile = p_sc.shape
    rep = tile // LANES
    m_sc[...] = jnp.full_like(m_sc, NEG_INF)
    acc_sc[...] = jnp.zeros_like(acc_sc)

    def update(slot, t, last):
        for c in range(rows_total // SOFTMAX_ROWS):
            rows = pl.ds(c * SOFTMAX_ROWS, SOFTMAX_ROWS)
            s = s_sc[slot, rows, :]
            if last:
                s = last_bias(s, c * SOFTMAX_ROWS)
            m_prev = m_sc[rows, :]
            m_new = jnp.maximum(m_prev, jnp.max(s, axis=-1, keepdims=True))
            a_sc[rows, :] = jnp.exp(m_prev - m_new)
            m_sc[rows, :] = m_new
            p_sc[rows, :] = jnp.exp(s - jnp.concatenate([m_new] * rep, axis=1)).astype(BF16)
        acc_sc[...] = a_sc[...] * acc_sc[...] + jnp.dot(p_sc[...], values(t), preferred_element_type=F32)

    s_sc[0] = logits(0)

    def two_tiles(u, carry):
        t = 2 * u
        s_sc[1] = logits(t + 1)
        update(0, t, False)
        s_sc[0] = logits(t + 2)
        update(1, t + 1, False)
        return carry

    lax.fori_loop(0, n_full // 2, two_tiles, 0)

    @pl.when(n_full % 2 == 1)
    def _():
        s_sc[1] = logits(n_full)
        update(0, n_full - 1, False)
        update(1, n_full, True)

    @pl.when(n_full % 2 == 0)
    def _():
        update(0, n_full, True)


def _softmax_scratch(rows, tile):
    return [pltpu.VMEM((2, rows, tile), F32), pltpu.VMEM((rows, tile), BF16),
            pltpu.VMEM((rows, LANES), F32), pltpu.VMEM((rows, LANES), F32), pltpu.VMEM((rows, LANES), F32)]


def _normalized(acc):
    return acc[:, :HEAD_DIM] / acc[:, HEAD_DIM:HEAD_DIM + 1]


def _fox_kernel(q_ref, k_ref, v_ref, o_ref, s_sc, p_sc, m_sc, a_sc, acc_sc):
    tile = KEY_TILE
    i = pl.program_id(2)
    q = q_ref[...]

    def logits(t):
        k0 = pl.multiple_of(t * tile, tile)
        return lax.dot_general(q, k_ref[pl.ds(k0, tile), :], NT_DIMS, preferred_element_type=F32)

    def values(t):
        return v_ref[pl.ds(pl.multiple_of(t * tile, tile), tile), :]

    def causal(s, first_row):
        qi = first_row + lax.broadcasted_iota(jnp.int32, s.shape, 0)
        ki = lax.broadcasted_iota(jnp.int32, s.shape, 1)
        return jnp.where(ki <= qi, s, NEG_INF)

    _online_softmax_tiles(i, logits, values, causal, s_sc, p_sc, m_sc, a_sc, acc_sc)
    o_ref[...] = _normalized(acc_sc[...]).astype(o_ref.dtype)


def fox_attention(qkv):
    _, b, h, s, _ = qkv.shape
    tile = KEY_TILE
    assert s % tile == 0 and ATTN_ROWS == tile
    whole = lambda plane: pl.BlockSpec((None, None, None, s, LANES), lambda b_, h_, i: (plane, b_, h_, 0, 0))
    return pl.pallas_call(
        _fox_kernel,
        out_shape=jax.ShapeDtypeStruct((b, h, s, HEAD_DIM), BF16),
        grid=(b, h, s // tile),
        in_specs=[pl.BlockSpec((None, None, None, tile, LANES), lambda b_, h_, i: (0, b_, h_, i, 0)),
                  whole(1), whole(2)],
        out_specs=pl.BlockSpec((None, None, tile, HEAD_DIM), lambda b_, h_, i: (b_, h_, i, 0)),
        scratch_shapes=_softmax_scratch(ATTN_ROWS, tile),
        compiler_params=_cparams(3), name="fox_attention",
    )(qkv, qkv, qkv)


def _gelu_tanh(x):
    return 0.5 * x * (1.0 + jnp.tanh(np.sqrt(2.0 / np.pi).astype(np.float32) * (x + 0.044715 * (x * x * x))))


def _compress_kernel(bk_ref, bv_ref, pek_ref, pev_ref, w1k_ref, w2k_ref, w1v_ref, w2v_ref,
                     kcols_ref, onecol_ref, ko_ref, vo_ref):
    def mlp(blk_ref, pe_ref, w1_ref, w2_ref):
        xin = (blk_ref[...] + pe_ref[...]).astype(BF16)
        hid = _gelu_tanh(jnp.dot(xin, w1_ref[...], preferred_element_type=F32))
        return jnp.dot(hid.astype(BF16), w2_ref[...], preferred_element_type=F32)

    ko_ref[...] = _widen(mlp(bk_ref, pek_ref, w1k_ref, w2k_ref), LANES) + kcols_ref[...]
    vo_ref[...] = _widen(mlp(bv_ref, pev_ref, w1v_ref, w2v_ref), LANES) + onecol_ref[...]


def compress_blocks(blk_k, blk_v, pe_k, pe_v, w1_k, w2_k, w1_v, w2_v, key_cols, one_col):
    b, g, nc, feat = blk_k.shape
    blk_spec = pl.BlockSpec((None, None, nc, feat), lambda i, j: (i, j, 0, 0))
    full = lambda shape: pl.BlockSpec(shape, lambda i, j: (0,) * len(shape))
    out = jax.ShapeDtypeStruct((b, g, nc, LANES), BF16)
    out_spec = pl.BlockSpec((None, None, nc, LANES), lambda i, j: (i, j, 0, 0))
    return pl.pallas_call(
        _compress_kernel,
        out_shape=(out, out),
        grid=(b, g),
        in_specs=[blk_spec, blk_spec, full((1, feat)), full((1, feat)),
                  full((feat, HEAD_DIM)), full((HEAD_DIM, HEAD_DIM)),
                  full((feat, HEAD_DIM)), full((HEAD_DIM, HEAD_DIM)),
                  full((nc, LANES)), full((1, LANES))],
        out_specs=(out_spec, out_spec),
        compiler_params=_cparams(2), name="nsa_compress",
    )(blk_k, blk_v, pe_k, pe_v, w1_k, w2_k, w1_v, w2_v, key_cols, one_col)


NSA_ROWS = NSA_HG * Q_BLOCK


def _select_blocks_t(score_ref, n_sel, n_top):
    groups = n_sel // SUBLANES
    grp = [score_ref[pl.ds(g * SUBLANES, SUBLANES), :] for g in range(groups)]
    blk = lax.broadcasted_iota(jnp.int32, (SUBLANES, LANES), 0)
    beaten_by = [jnp.zeros((SUBLANES, LANES), F32) for _ in range(groups)]
    for k in range(n_sel):
        sk = jnp.broadcast_to(score_ref[pl.ds(k, 1), :], (SUBLANES, LANES))
        for g in range(groups):
            if g * SUBLANES > k:
                wins = sk >= grp[g]
            elif g * SUBLANES + SUBLANES - 1 < k:
                wins = sk > grp[g]
            else:
                wins = (sk > grp[g]) | ((sk == grp[g]) & (blk + g * SUBLANES > k))
            beaten_by[g] = beaten_by[g] + jnp.where(wins, 1.0, 0.0)
    return jnp.concatenate([jnp.where(c < n_top, 1.0, 0.0) for c in beaten_by], axis=0)


def _nsa_kernel(q_ref, kc_ref, vc_ref, ks_ref, vs_ref, kw_ref, vw_ref, gl_ref, selmap_ref, o_ref,
                score_sc, qa_sc, s_sc, p_sc, m_sc, a_sc, acc_sc, *, seq, n_top):
    i = pl.program_id(2)
    q0 = i * Q_BLOCK
    n_cmp_pad = seq // CMP_STRIDE
    n_sel = seq // SEL_LEN
    q1 = q_ref[...]
    row = lax.broadcasted_iota(jnp.int32, (NSA_ROWS, 1), 0)
    qpos_col = q0 + (row & (Q_BLOCK - 1))

    cend = lax.broadcasted_iota(jnp.int32, (1, n_cmp_pad), 1) * CMP_STRIDE + (CMP_LEN - 1)
    valid_c = cend <= qpos_col
    lc = jnp.where(valid_c, lax.dot_general(q1, kc_ref[...], NT_DIMS, preferred_element_type=F32), NEG_INF)
    mc = jnp.max(lc, axis=-1, keepdims=True)
    pc = jnp.where(valid_c, jnp.exp(lc - mc), 0.0)
    sc = jnp.sum(pc, axis=-1, keepdims=True)
    pc = pc / jnp.where(sc > 0, sc, 1.0)
    o_c = jnp.dot(pc.astype(BF16), vc_ref[...], preferred_element_type=F32)[:, :HEAD_DIM]

    pcs = pc[0:Q_BLOCK]
    for hh in range(1, NSA_HG):
        pcs = pcs + pc[hh * Q_BLOCK:(hh + 1) * Q_BLOCK]
    hi = pcs.astype(BF16)
    lo = (pcs - hi.astype(F32)).astype(BF16)
    selmap = selmap_ref[...]
    imp_t = (lax.dot_general(selmap, hi, NT_DIMS, preferred_element_type=F32)
             + lax.dot_general(selmap, lo, NT_DIMS, preferred_element_type=F32))
    blk = lax.broadcasted_iota(jnp.int32, (n_sel, Q_BLOCK), 0)
    qrow = q0 + lax.broadcasted_iota(jnp.int32, (n_sel, Q_BLOCK), 1)
    qblk = qrow >> 6
    forced = (blk == 0) | (blk == qblk) | (blk == qblk - 1)
    score_sc[...] = jnp.where(blk * SEL_LEN <= qrow, imp_t + jnp.where(forced, FORCE_BONUS, 0.0), NEG_INF)
    sel_t = _select_blocks_t(score_sc, n_sel, n_top)

    sel_t = jnp.concatenate([sel_t, jnp.ones((LANES - n_sel, Q_BLOCK), F32)], axis=0).astype(BF16)
    eye = jnp.where(lax.broadcasted_iota(jnp.int32, (Q_BLOCK, Q_BLOCK), 0)
                    == lax.broadcasted_iota(jnp.int32, (Q_BLOCK, Q_BLOCK), 1), 1.0, 0.0).astype(BF16)
    sel = lax.dot_general(eye, sel_t, NT_DIMS, preferred_element_type=F32)
    closed = jnp.where(sel > 0.5, 0.0, NEG_INF).astype(BF16)
    qa_sc[:, 0:LANES] = q1
    for hh in range(NSA_HG):
        qa_sc[hh * Q_BLOCK:(hh + 1) * Q_BLOCK, LANES:2 * LANES] = closed

    band = WIN + Q_BLOCK
    w0 = pl.multiple_of(jnp.maximum(q0 - WIN, 0), Q_BLOCK)
    rel_w = (w0 + lax.broadcasted_iota(jnp.int32, (Q_BLOCK, band), 1)
             - (q0 + lax.broadcasted_iota(jnp.int32, (Q_BLOCK, band), 0)))
    bias_w = jnp.where((rel_w <= 0) & (rel_w > -WIN), 0.0, NEG_INF)
    sw = lax.dot_general(q1, kw_ref[pl.ds(w0, band), :], NT_DIMS, preferred_element_type=F32)
    sw = (sw.reshape(NSA_HG, Q_BLOCK, band) + bias_w[None]).reshape(NSA_ROWS, band)
    pw = jnp.exp(sw - jnp.max(sw, axis=-1, keepdims=True))
    o_w = _normalized(jnp.dot(pw.astype(BF16), vw_ref[pl.ds(w0, band), :], preferred_element_type=F32))
    gate = _sigmoid(gl_ref[...])
    out_cw = gate[:, 0:1] * o_c + gate[:, 2:3] * o_w

    tile = KEY_TILE
    n_full = q0 // tile

    def logits(t):
        k0 = pl.multiple_of(t * tile, tile)
        return lax.dot_general(qa_sc[...], ks_ref[pl.ds(k0, tile), :], NT_DIMS, preferred_element_type=F32)

    def values(t):
        return vs_ref[pl.ds(pl.multiple_of(t * tile, tile), tile), :]

    def causal(s, first_row):
        r = first_row + lax.broadcasted_iota(jnp.int32, s.shape, 0)
        kpos = n_full * tile + lax.broadcasted_iota(jnp.int32, s.shape, 1)
        return jnp.where(kpos <= q0 + (r & (Q_BLOCK - 1)), s, NEG_INF)

    _online_softmax_tiles(n_full, logits, values, causal, s_sc, p_sc, m_sc, a_sc, acc_sc)
    o_s = _normalized(acc_sc[...])
    o_ref[...] = (out_cw + gate[:, 1:2] * o_s).astype(o_ref.dtype)


def nsa_attention(q_aug, kc_aug, vc_aug, ks_aug, vs_aug, kw_aug, vw_aug, gate_logit, selmap):
    b, g, nb = q_aug.shape[:3]
    s = vs_aug.shape[2]
    nc = kc_aug.shape[2]
    n_sel = s // SEL_LEN
    assert s % KEY_TILE == 0 and s >= WIN + Q_BLOCK and n_sel <= LANES and NSA_ROWS == ATTN_ROWS
    per_block = lambda width: pl.BlockSpec((None, None, None, NSA_ROWS, width),
                                           lambda b_, g_, i: (b_, g_, i, 0, 0))
    per_group = lambda rows, width: pl.BlockSpec((None, None, rows, width), lambda b_, g_, i: (b_, g_, 0, 0))
    return pl.pallas_call(
        functools.partial(_nsa_kernel, seq=s, n_top=min(SEL_TOP, n_sel)),
        out_shape=jax.ShapeDtypeStruct((b, g, nb, NSA_ROWS, HEAD_DIM), BF16),
        grid=(b, g, nb),
        in_specs=[per_block(LANES), per_group(nc, LANES), per_group(nc, LANES),
                  per_group(s, 2 * LANES), per_group(s, LANES), per_group(s, LANES), per_group(s, LANES),
                  per_block(3), pl.BlockSpec((n_sel, nc), lambda b_, g_, i: (0, 0))],
        out_specs=per_block(HEAD_DIM),
        scratch_shapes=[pltpu.VMEM((n_sel, Q_BLOCK), F32), pltpu.VMEM((NSA_ROWS, 2 * LANES), BF16)]
        + _softmax_scratch(NSA_ROWS, KEY_TILE),
        compiler_params=_cparams(3), name="nsa_attention",
    )(q_aug, kc_aug, vc_aug, ks_aug, vs_aug, kw_aug, vw_aug, gate_logit, selmap)


def _causal_conv3(u, halo, w):
    row = lax.broadcasted_iota(jnp.int32, u.shape, 0)
    last = halo.shape[0] - 1
    prev1 = halo[last:last + 1]
    prev2 = halo[last - 1:last]
    u1 = jnp.where(row == 0, prev1, pltpu.roll(u, 1, 0))
    u2 = jnp.where(row == 0, prev2, jnp.where(row == 1, prev1, pltpu.roll(u, 2, 0)))
    return w[0:1] * u2 + w[1:2] * u1 + w[2:3] * u


MIX_ROWS = 512
MIX_COLS = 256


def _mixer_output_kernel(h_ref, wg_ref, bg_ref, cb_ref, cc_ref, cx_ref, cch_ref, cxh_ref, a_ref, c_ref,
                         wa_ref, wb_ref, wc_ref, cw_ref, wo_ref, x_ref, g_ref, xo_ref, ho_ref,
                         a_sc, b_sc, c_sc, mix_sc, *, tm, seq):
    d = x_ref.shape[1]
    for hh in range(FOX_HEADS):
        a_sc[:, hh * HEAD_DIM:(hh + 1) * HEAD_DIM] = a_ref[hh]
    for g in range(NSA_KV_GROUPS):
        for hg in range(NSA_HG):
            c0 = (g * NSA_HG + hg) * HEAD_DIM
            for qb in range(tm // Q_BLOCK):
                c_sc[qb * Q_BLOCK:(qb + 1) * Q_BLOCK, c0:c0 + HEAD_DIM] = c_ref[g, qb, hg * Q_BLOCK:(hg + 1) * Q_BLOCK, :]
    seq_start = (pl.program_id(0) * tm) % seq == 0
    u = cc_ref[...] * cx_ref[...]
    halo = jnp.where(seq_start, 0.0, cch_ref[...] * cxh_ref[...])
    b_sc[...] = (cb_ref[...] * _causal_conv3(u, halo, cw_ref[...])).astype(BF16)

    h = h_ref[...]
    for col in range(d // MIX_COLS):
        cols = slice(col * MIX_COLS, (col + 1) * MIX_COLS)
        mix = jnp.zeros((tm, MIX_COLS), F32)
        for branch, (src, w_ref) in enumerate(((a_sc, wa_ref), (b_sc, wb_ref), (c_sc, wc_ref))):
            gcols = slice(branch * d + col * MIX_COLS, branch * d + (col + 1) * MIX_COLS)
            gate = jnp.dot(h, wg_ref[:, gcols], preferred_element_type=F32) + bg_ref[:, gcols]
            mix = mix + _sigmoid(gate) * jnp.dot(src[...], w_ref[:, cols], preferred_element_type=F32)
        mix_sc[:, cols] = mix.astype(BF16)

    xn = x_ref[...] + jnp.dot(mix_sc[...], wo_ref[...], preferred_element_type=F32)
    xo_ref[...] = xn
    ho_ref[...] = _rms(xn, g_ref[...]).astype(ho_ref.dtype)


def mixer_output(h, w_gate, b_gate, zc, a_out, c_out, w_a, w_b, w_c, conv_w, w_o, x, g, seq):
    n, d = x.shape
    cw = w_b.shape[0]
    tm = min(MIX_ROWS, seq)
    tpb = seq // tm
    nb = tm // Q_BLOCK
    row = lambda width: pl.BlockSpec((tm, width), lambda i: (i, 0))
    ccol = lambda c: pl.BlockSpec((tm, cw), lambda i: (i, c))
    hcol = lambda c: pl.BlockSpec((SUBLANES, cw), lambda i: (jnp.maximum(i * (tm // SUBLANES) - 1, 0), c))
    once = lambda shape: pl.BlockSpec(shape, lambda i: (0, 0), pipeline_mode=pl.Buffered(1))
    return pl.pallas_call(
        functools.partial(_mixer_output_kernel, tm=tm, seq=seq),
        out_shape=(jax.ShapeDtypeStruct((n, d), F32), jax.ShapeDtypeStruct((n, d), BF16)),
        grid=(n // tm,),
        in_specs=[row(d), once(w_gate.shape), once((1, 3 * d)),
                  ccol(0), ccol(1), ccol(2), hcol(1), hcol(2),
                  pl.BlockSpec((None, FOX_HEADS, tm, HEAD_DIM), lambda i: (i // tpb, 0, i % tpb, 0)),
                  pl.BlockSpec((None, NSA_KV_GROUPS, nb, NSA_ROWS, HEAD_DIM), lambda i: (i // tpb, 0, i % tpb, 0, 0)),
                  once(w_a.shape), once(w_b.shape), once(w_c.shape), once(conv_w.shape), once(w_o.shape),
                  row(d), once((1, d))],
        out_specs=(row(d), row(d)),
        scratch_shapes=[pltpu.VMEM((tm, FOX_HEADS * HEAD_DIM), BF16), pltpu.VMEM((tm, cw), BF16),
                        pltpu.VMEM((tm, NSA_HEADS * HEAD_DIM), BF16), pltpu.VMEM((tm, d), BF16)],
        compiler_params=_cparams(1), name="mixer_output",
    )(h, w_gate, b_gate.reshape(1, -1), zc, zc, zc, zc, zc, a_out, c_out, w_a, w_b, w_c, conv_w, w_o,
      x, g.reshape(1, d))


def _ffn_up_kernel(h_ref, hh_ref, wu_ref, wv_ref, cw_ref, o_ref, *, tm, seq):
    seq_start = (pl.program_id(1) * tm) % seq == 0
    wu = wu_ref[...]
    u = jnp.dot(h_ref[...], wu, preferred_element_type=F32)
    v = jnp.dot(h_ref[...], wv_ref[...], preferred_element_type=F32)
    halo = jnp.where(seq_start, 0.0, jnp.dot(hh_ref[...], wu, preferred_element_type=F32))
    c = _causal_conv3(u, halo, cw_ref[...])
    o_ref[...] = (c * _sigmoid(c) * v).astype(o_ref.dtype)


def ffn_up(h, w_up, conv_w, seq, tm=512):
    n, d = h.shape
    f = conv_w.shape[1]
    tm = min(tm, n)
    nj = 2 if (f // 2) % LANES == 0 else 1
    tn = f // nj
    return pl.pallas_call(
        functools.partial(_ffn_up_kernel, tm=tm, seq=seq),
        out_shape=jax.ShapeDtypeStruct((n, f), BF16),
        grid=(nj, n // tm),
        in_specs=[pl.BlockSpec((tm, d), lambda j, i: (i, 0)),
                  pl.BlockSpec((2 * SUBLANES, d), lambda j, i: (jnp.maximum(i * (tm // (2 * SUBLANES)) - 1, 0), 0)),
                  pl.BlockSpec((d, tn), lambda j, i: (0, j)),
                  pl.BlockSpec((d, tn), lambda j, i: (0, nj + j)),
                  pl.BlockSpec((conv_w.shape[0], tn), lambda j, i: (0, j))],
        out_specs=pl.BlockSpec((tm, tn), lambda j, i: (i, j)),
        compiler_params=_cparams(2), name="ffn_up",
    )(h, h, w_up, w_up, conv_w)


def _selection_map(n_sel, n_cmp_pad):
    sel_start = np.arange(n_sel)[:, None] * SEL_LEN
    cmp_start = np.arange(n_cmp_pad)[None, :] * CMP_STRIDE
    ov = np.minimum(sel_start + SEL_LEN, cmp_start + CMP_LEN) - np.maximum(sel_start, cmp_start)
    ov = np.clip(ov, 0, None) / CMP_STRIDE
    ov[:, n_cmp_pad - 1] = 0.0
    return ov.astype(np.float32)


def _bf16_exact(a):
    a = np.asarray(a, np.float32)
    assert np.array_equal(a, a.astype(BF16).astype(np.float32)), "constant is not exact in bf16"
    return a


def _after_head(cols, width=LANES):
    out = np.zeros(cols.shape[:-1] + (width,), np.float32)
    out[..., HEAD_DIM:HEAD_DIM + cols.shape[-1]] = cols
    return out


def _alibi_query_cols(nqb):
    slope = np.array([2.0 ** (-8.0 * (i + 1) / NSA_HEADS) for i in range(NSA_HEADS)], np.float32)
    slope = slope.reshape(NSA_KV_GROUPS, 1, NSA_HG, 1)
    qpos = (np.arange(nqb)[:, None] * Q_BLOCK + np.arange(Q_BLOCK)[None, :]).reshape(1, nqb, 1, Q_BLOCK)
    cols = np.stack(np.broadcast_arrays(slope * SEL_LEN, slope, -slope * SEL_LEN * (qpos // SEL_LEN),
                                        -slope * (qpos % SEL_LEN)), axis=-1)
    return _bf16_exact(cols.reshape(NSA_KV_GROUPS, nqb, NSA_ROWS, 4))


def _alibi_key_cols(pos):
    pos = np.asarray(pos)
    return _bf16_exact(np.stack([pos // SEL_LEN, pos % SEL_LEN, np.ones_like(pos), np.ones_like(pos)], axis=-1))


def _heads_major(t, heads):
    b, s, _ = t.shape
    return t.reshape(b, s, heads, HEAD_DIM).transpose(0, 2, 1, 3)


def _overlapping_blocks(t):
    b, s, _ = t.shape
    chunks = _heads_major(t, NSA_KV_GROUPS).reshape(b, NSA_KV_GROUPS, s // CMP_STRIDE, CMP_STRIDE * HEAD_DIM)
    nxt = jnp.concatenate([chunks[:, :, 1:], jnp.zeros_like(chunks[:, :, :1])], axis=2)
    return jnp.concatenate([chunks, nxt], axis=-1)


def _split_w_in(w_in_l, b_in_l, d):
    sizes = (d, d, d, 512, 512, 512, FOX_HEADS, 512, 512, 512, 512, 128, 128, 128, 128, 128, 128, NSA_HEADS * 3)
    offs = np.concatenate([[0], np.cumsum(sizes)])
    col = lambda a, k: a[..., int(offs[k]):int(offs[k + 1])]
    (GA, GB, GC, FQ, FK, FV, FF, CB, CC, CX, NQ, NKC, NVC, NKS, NVS, NKW, NVW, NG) = range(18)
    pad = 384 - (128 + 128 + FOX_HEADS + NSA_HEADS * 3)

    prescale = {FQ: SCALE, NQ: SCALE}

    def group(idx, extra=0):
        w = jnp.concatenate([col(w_in_l, k) * prescale.get(k, 1.0) for k in idx], axis=-1)
        bias = jnp.concatenate([col(b_in_l, k) * prescale.get(k, 1.0) for k in idx], axis=-1)
        if extra:
            w = jnp.pad(w, ((0, 0), (0, extra)))
            bias = jnp.pad(bias, ((0, extra),))
        return w.astype(BF16), bias

    fox = [group((k,)) for k in (FQ, FK, FV)]
    return ((jnp.stack([w for w, _ in fox]), jnp.stack([b_[None] for _, b_ in fox])),
            group((NQ,)), group((NKS, NVS, NKW, NVW)),
            group((GA, GB, GC)), group((CB, CC, CX)), group((NKC, NVC, FF, NG), pad))


def kernel(x, norm_mix_g, w_in, b_in, cmp_pe_k, cmp_w1_k, cmp_w2_k, cmp_pe_v, cmp_w1_v, cmp_w2_v, sc_conv_w, w_br_a, w_br_b, w_br_c, w_o, norm_ffn_g, w_up, ffn_conv_w, w_down, norm_final_g):
    b, s, d = x.shape
    depth = w_in.shape[0]
    n = b * s
    nqb = s // Q_BLOCK
    n_cmp_pad = s // CMP_STRIDE
    const = lambda a: jnp.asarray(a, dtype=BF16)
    selmap = const(_selection_map(s // SEL_LEN, n_cmp_pad))
    one_col = const(_after_head(np.ones((1, 1), np.float32)))
    q_cols = const(_after_head(_alibi_query_cols(nqb)))
    cmp_cols = const(_after_head(_alibi_key_cols(np.arange(n_cmp_pad) * CMP_STRIDE + CMP_LEN - 1)))
    key_cols = _after_head(_alibi_key_cols(np.arange(s)))
    key_block = (np.arange(s)[:, None] // SEL_LEN == np.arange(LANES)[None, :]).astype(np.float32)
    sel_cols = const(np.concatenate([key_cols, key_block], axis=-1))
    key_cols = const(key_cols)

    xr = x.reshape(n, d)
    h = rmsnorm_rows(xr, norm_mix_g[0], BF16)
    for l in range(depth):
        ((w_fox, b_fox), (w_nq, b_nq), (w_nkv, b_nkv), (w_gate, b_gate), (w_conv, b_conv),
         (w_sm, b_sm)) = _split_w_in(w_in[l], b_in[l], d)
        zc = matmul_bias(h, w_conv, b_conv, F32, tn=512)
        zs3 = matmul_bias(h, w_sm, b_sm, F32, tn=384).reshape(b, s, -1)

        bias_cols = fox_bias_columns(zs3[..., 256:256 + FOX_HEADS].transpose(0, 2, 1))
        a_out = fox_attention(fox_qkv_projection(h, w_fox, b_fox, bias_cols, b, s))

        kc_aug, vc_aug = compress_blocks(
            _overlapping_blocks(zs3[..., 0:128]), _overlapping_blocks(zs3[..., 128:256]),
            cmp_pe_k[l].reshape(1, -1), cmp_pe_v[l].reshape(1, -1),
            cmp_w1_k[l].astype(BF16), cmp_w2_k[l].astype(BF16),
            cmp_w1_v[l].astype(BF16), cmp_w2_v[l].astype(BF16), cmp_cols, one_col)
        ng = zs3[..., 264:264 + NSA_HEADS * 3].reshape(b, nqb, Q_BLOCK, NSA_KV_GROUPS, NSA_HG, 3)
        ng = ng.transpose(0, 3, 1, 4, 2, 5).reshape(b, NSA_KV_GROUPS, nqb, NSA_ROWS, 3)
        ks_aug, vs_aug, kw_aug, vw_aug = nsa_kv_projection(h, w_nkv, b_nkv, sel_cols, key_cols, one_col, b, s)
        c_out = nsa_attention(nsa_q_projection(h, w_nq, b_nq, q_cols, b, s), kc_aug, vc_aug,
                              ks_aug, vs_aug, kw_aug, vw_aug, ng, selmap)

        xr, h2 = mixer_output(h, w_gate, b_gate, zc, a_out, c_out, w_br_a[l].astype(BF16),
                              w_br_b[l].astype(BF16), w_br_c[l].astype(BF16), sc_conv_w[l],
                              w_o[l].astype(BF16), xr, norm_ffn_g[l], s)

        act = ffn_up(h2, w_up[l].astype(BF16), ffn_conv_w[l], s)
        if l + 1 < depth:
            xr, h = residual_matmul_norm(act, w_down[l].astype(BF16), xr, norm_mix_g[l + 1], BF16)
        else:
            out = residual_matmul_norm(act, w_down[l].astype(BF16), xr, norm_final_g, F32,
                                       keep_residual=False)
    return out.reshape(b, s, d)
```

```python
import functools

import numpy as np
import jax
import jax.numpy as jnp
from jax import lax
from jax.experimental import pallas as pl
from jax.experimental.pallas import tpu as pltpu

F32 = jnp.float32
BF16 = jnp.bfloat16

HEAD_DIM = 64
FOX_HEADS = 8
NSA_HEADS = 8
NSA_KV_GROUPS = 2
NSA_HG = NSA_HEADS // NSA_KV_GROUPS
CMP_LEN = 32
CMP_STRIDE = 16
SEL_LEN = 64
SEL_TOP = 16
WIN = 512
Q_BLOCK = 128
EPS = 1e-6
NEG_INF = -1e30
FORCE_BONUS = 1e4
SCALE = HEAD_DIM ** -0.5

LANES = 128
SUBLANES = 8
VMEM_LIMIT = 48 * 1024 * 1024
PROJ_ROWS = 1024

NT_DIMS = (((1,), (1,)), ((), ()))


def _cparams(n_parallel, n_arbitrary=0):
    return pltpu.CompilerParams(
        dimension_semantics=("parallel",) * n_parallel + ("arbitrary",) * n_arbitrary,
        vmem_limit_bytes=VMEM_LIMIT)


def _sigmoid(x):
    return 1.0 / (1.0 + jnp.exp(-x))


def _rms(x, g):
    ms = jnp.mean(x * x, axis=-1, keepdims=True)
    return x * lax.rsqrt(ms + EPS) * g


def _widen(x, width):
    return jnp.concatenate([x, jnp.zeros((x.shape[0], width - x.shape[1]), x.dtype)], axis=1).astype(BF16)


def _norm_kernel(x_ref, g_ref, o_ref):
    o_ref[...] = _rms(x_ref[...], g_ref[...]).astype(o_ref.dtype)


def rmsnorm_rows(x, g, out_dtype, tm=1024):
    n, d = x.shape
    tm = min(tm, n)
    return pl.pallas_call(
        _norm_kernel,
        out_shape=jax.ShapeDtypeStruct((n, d), out_dtype),
        grid=(n // tm,),
        in_specs=[pl.BlockSpec((tm, d), lambda i: (i, 0)),
                  pl.BlockSpec((1, d), lambda i: (0, 0))],
        out_specs=pl.BlockSpec((tm, d), lambda i: (i, 0)),
        compiler_params=_cparams(1),
        name="rmsnorm",
    )(x, g.reshape(1, d))


def _mm_bias_kernel(a_ref, w_ref, b_ref, o_ref):
    acc = jnp.dot(a_ref[...], w_ref[...], preferred_element_type=F32)
    o_ref[...] = (acc + b_ref[...]).astype(o_ref.dtype)


def matmul_bias(a, w, b, out_dtype, tn, tm=PROJ_ROWS):
    n, k = a.shape
    m = w.shape[1]
    tm = min(tm, n)
    return pl.pallas_call(
        _mm_bias_kernel,
        out_shape=jax.ShapeDtypeStruct((n, m), out_dtype),
        grid=(n // tm, m // tn),
        in_specs=[pl.BlockSpec((tm, k), lambda i, j: (i, 0)),
                  pl.BlockSpec((k, tn), lambda i, j: (0, j)),
                  pl.BlockSpec((1, tn), lambda i, j: (0, j))],
        out_specs=pl.BlockSpec((tm, tn), lambda i, j: (i, j)),
        compiler_params=_cparams(2),
        name="matmul_bias",
    )(a, w, b.reshape(1, m))


HEAD_PROJ_ROWS = 512


def _fox_qkv_kernel(a_ref, w_ref, b_ref, aug_ref, o_ref):
    acc = jnp.dot(a_ref[...], w_ref[...], preferred_element_type=F32) + b_ref[...]
    lane = lax.broadcasted_iota(jnp.int32, (acc.shape[0], LANES), 1)
    ones_col = jnp.where(lane == HEAD_DIM, 1.0, 0.0).astype(BF16)
    for plane in range(3):
        for hh in range(FOX_HEADS):
            c0 = (plane * FOX_HEADS + hh) * HEAD_DIM
            extra = ones_col if plane == 2 else aug_ref[plane, hh]
            o_ref[plane, hh] = _widen(acc[:, c0:c0 + HEAD_DIM], LANES) + extra


def fox_qkv_projection(h, w, b, aug, batch, seq):
    n, d = h.shape
    tm = min(HEAD_PROJ_ROWS, seq)
    tpb = seq // tm
    plane_block = lambda planes: pl.BlockSpec((planes, None, FOX_HEADS, tm, LANES),
                                              lambda i: (0, i // tpb, 0, i % tpb, 0))
    return pl.pallas_call(
        _fox_qkv_kernel,
        out_shape=jax.ShapeDtypeStruct((3, batch, FOX_HEADS, seq, LANES), BF16),
        grid=(n // tm,),
        in_specs=[pl.BlockSpec((tm, d), lambda i: (i, 0)),
                  pl.BlockSpec(w.shape, lambda i: (0, 0)),
                  pl.BlockSpec((1, w.shape[1]), lambda i: (0, 0)),
                  plane_block(2)],
        out_specs=plane_block(3),
        compiler_params=_cparams(1), name="fox_qkv_projection",
    )(h, w, b.reshape(1, -1), aug)


def _nsa_proj_kernel(a_ref, w_ref, b_ref, qcols_ref, selcols_ref, keycols_ref, onecol_ref,
                     q_ref, ks_ref, vs_ref, kw_ref, vw_ref, cmp_ref):
    acc = jnp.dot(a_ref[...], w_ref[...], preferred_element_type=F32) + b_ref[...]
    for g in range(NSA_KV_GROUPS):
        for hg in range(NSA_HG):
            c0 = (g * NSA_HG + hg) * HEAD_DIM
            for qb in range(acc.shape[0] // Q_BLOCK):
                rows = slice(hg * Q_BLOCK, (hg + 1) * Q_BLOCK)
                q_ref[g, qb, rows, :] = (_widen(acc[qb * Q_BLOCK:(qb + 1) * Q_BLOCK, c0:c0 + HEAD_DIM], LANES)
                                         + qcols_ref[g, qb, rows, :])
    gw = NSA_KV_GROUPS * HEAD_DIM
    kv0 = NSA_HEADS * HEAD_DIM
    for g in range(NSA_KV_GROUPS):
        part = lambda idx: acc[:, kv0 + idx * gw + g * HEAD_DIM: kv0 + idx * gw + (g + 1) * HEAD_DIM]
        ks_ref[g] = _widen(part(0), 2 * LANES) + selcols_ref[...]
        vs_ref[g] = _widen(part(1), LANES) + onecol_ref[...]
        kw_ref[g] = _widen(part(2), LANES) + keycols_ref[...]
        vw_ref[g] = _widen(part(3), LANES) + onecol_ref[...]
        cmp_ref[0, g] = part(4)
        cmp_ref[1, g] = part(5)


def nsa_projection(h, w, b, q_cols, sel_cols, key_cols, one_col, batch, seq):
    n, d = h.shape
    tm = min(HEAD_PROJ_ROWS, seq)
    tpb = seq // tm
    nb = tm // Q_BLOCK
    rows = NSA_HG * Q_BLOCK
    out = lambda width: jax.ShapeDtypeStruct((batch, NSA_KV_GROUPS, seq, width), BF16)
    out_spec = lambda width: pl.BlockSpec((None, NSA_KV_GROUPS, tm, width), lambda i: (i // tpb, 0, i % tpb, 0))
    return pl.pallas_call(
        _nsa_proj_kernel,
        out_shape=(jax.ShapeDtypeStruct((batch, NSA_KV_GROUPS, seq // Q_BLOCK, rows, LANES), BF16),
                   out(2 * LANES), out(LANES), out(LANES), out(LANES),
                   jax.ShapeDtypeStruct((2, batch, NSA_KV_GROUPS, seq, HEAD_DIM), F32)),
        grid=(n // tm,),
        in_specs=[pl.BlockSpec((tm, d), lambda i: (i, 0)),
                  pl.BlockSpec(w.shape, lambda i: (0, 0)),
                  pl.BlockSpec((1, w.shape[1]), lambda i: (0, 0)),
                  pl.BlockSpec((NSA_KV_GROUPS, nb, rows, LANES), lambda i: (0, i % tpb, 0, 0)),
                  pl.BlockSpec((tm, 2 * LANES), lambda i: (i % tpb, 0)),
                  pl.BlockSpec((tm, LANES), lambda i: (i % tpb, 0)),
                  pl.BlockSpec((1, LANES), lambda i: (0, 0))],
        out_specs=(pl.BlockSpec((None, NSA_KV_GROUPS, nb, rows, LANES), lambda i: (i // tpb, 0, i % tpb, 0, 0)),
                   out_spec(2 * LANES), out_spec(LANES), out_spec(LANES), out_spec(LANES),
                   pl.BlockSpec((2, None, NSA_KV_GROUPS, tm, HEAD_DIM), lambda i: (0, i // tpb, 0, i % tpb, 0))),
        compiler_params=_cparams(1), name="nsa_projection",
    )(h, w, b.reshape(1, -1), q_cols, sel_cols, key_cols, one_col)


def _res_norm_kernel(a_ref, w_ref, x_ref, g_ref, xo_ref, ho_ref):
    xn = x_ref[...] + jnp.dot(a_ref[...], w_ref[...], preferred_element_type=F32)
    xo_ref[...] = xn
    ho_ref[...] = _rms(xn, g_ref[...]).astype(ho_ref.dtype)


def _res_norm_only_kernel(a_ref, w_ref, x_ref, g_ref, ho_ref):
    xn = x_ref[...] + jnp.dot(a_ref[...], w_ref[...], preferred_element_type=F32)
    ho_ref[...] = _rms(xn, g_ref[...]).astype(ho_ref.dtype)


def residual_matmul_norm(a, w, x, g, norm_dtype, keep_residual=True, tm=512):
    n, k = a.shape
    d = w.shape[1]
    tm = min(tm, n)
    row = lambda i: (i, 0)
    in_specs = [pl.BlockSpec((tm, k), row),
                pl.BlockSpec((k, d), lambda i: (0, 0)),
                pl.BlockSpec((tm, d), row),
                pl.BlockSpec((1, d), lambda i: (0, 0))]
    if keep_residual:
        return pl.pallas_call(
            _res_norm_kernel,
            out_shape=(jax.ShapeDtypeStruct((n, d), F32), jax.ShapeDtypeStruct((n, d), norm_dtype)),
            grid=(n // tm,), in_specs=in_specs,
            out_specs=(pl.BlockSpec((tm, d), row), pl.BlockSpec((tm, d), row)),
            compiler_params=_cparams(1), name="residual_matmul_norm",
        )(a, w, x, g.reshape(1, d))
    return pl.pallas_call(
        _res_norm_only_kernel,
        out_shape=jax.ShapeDtypeStruct((n, d), norm_dtype),
        grid=(n // tm,), in_specs=in_specs,
        out_specs=pl.BlockSpec((tm, d), row),
        compiler_params=_cparams(1), name="residual_matmul_final_norm",
    )(a, w, x, g.reshape(1, d))


CUM_BLOCK = 256


def _bf16_pieces(x):
    hi = x.astype(BF16).astype(F32)
    r1 = x - hi
    mid = r1.astype(BF16).astype(F32)
    lo = (r1 - mid).astype(BF16).astype(F32)
    return hi, mid, lo


CUM_CHUNK = 1024


def _fox_bias_columns_kernel(f_ref, o_ref, carry_sc):
    heads, chunk = f_ref.shape

    @pl.when(pl.program_id(1) == 0)
    def _():
        carry_sc[...] = jnp.zeros_like(carry_sc)

    x = f_ref[...]
    ls = jnp.minimum(x, 0.0) - jnp.log1p(jnp.exp(-jnp.abs(x)))
    r = lax.broadcasted_iota(jnp.int32, (CUM_BLOCK, CUM_BLOCK), 0)
    c = lax.broadcasted_iota(jnp.int32, (CUM_BLOCK, CUM_BLOCK), 1)
    tri = jnp.where(r <= c, 1.0, 0.0).astype(BF16)
    eye = jnp.where(r == c, 1.0, 0.0).astype(BF16)
    lane_q = lax.broadcasted_iota(jnp.int32, (2 * LANES, CUM_BLOCK), 0) - HEAD_DIM
    lane_k = lane_q - LANES
    ones_rows = jnp.where(((lane_q >= 3) & (lane_q < 6)) | ((lane_k >= 0) & (lane_k < 3)), 1.0, 0.0)
    carry = carry_sc[...]
    for blk in range(chunk // CUM_BLOCK):
        cols = slice(blk * CUM_BLOCK, (blk + 1) * CUM_BLOCK)
        cs = carry
        for piece in _bf16_pieces(ls[:, cols]):
            cs = cs + jnp.dot(piece.astype(BF16), tri, preferred_element_type=F32)
        carry = cs[:, CUM_BLOCK - 1:CUM_BLOCK]
        pieces = _bf16_pieces(cs)
        for hh in range(heads):
            both = ones_rows
            for k, piece in enumerate(pieces):
                both = jnp.where(lane_q == k, piece[hh:hh + 1], both)
                both = jnp.where(lane_k == 3 + k, -piece[hh:hh + 1], both)
            cols_t = lax.dot_general(eye, both.astype(BF16), NT_DIMS, preferred_element_type=F32)
            o_ref[0, hh, cols, :] = cols_t[:, :LANES].astype(o_ref.dtype)
            o_ref[1, hh, cols, :] = cols_t[:, LANES:].astype(o_ref.dtype)
    carry_sc[...] = carry


def fox_bias_columns(f_t):
    b, h, s = f_t.shape
    chunk = min(CUM_CHUNK, s)
    return pl.pallas_call(
        _fox_bias_columns_kernel,
        out_shape=jax.ShapeDtypeStruct((2, b, h, s, LANES), BF16),
        grid=(b, s // chunk),
        in_specs=[pl.BlockSpec((None, h, chunk), lambda i, j: (i, 0, j))],
        out_specs=pl.BlockSpec((2, None, h, chunk, LANES), lambda i, j: (0, i, 0, j, 0)),
        scratch_shapes=[pltpu.VMEM((h, 1), F32)],
        compiler_params=_cparams(1, 1), name="fox_bias_columns",
    )(f_t)


KEY_TILE = 512
ATTN_ROWS = 512
SOFTMAX_ROWS = 64


def _online_softmax_tiles(n_full, logits, values, last_bias, s_sc, p_sc, m_sc, a_sc, acc_sc):
    rows_total, tile = p_sc.shape
    rep = tile // LANES
    m_sc[...] = jnp.full_like(m_sc, NEG_INF)
    acc_sc[...] = jnp.zeros_like(acc_sc)

    def update(slot, t, last):
        for c in range(rows_total // SOFTMAX_ROWS):
            rows = pl.ds(c * SOFTMAX_ROWS, SOFTMAX_ROWS)
            s = s_sc[slot, rows, :]
            if last:
                s = last_bias(s, c * SOFTMAX_ROWS)
            m_prev = m_sc[rows, :]
            m_new = jnp.maximum(m_prev, jnp.max(s, axis=-1, keepdims=True))
            a_sc[rows, :] = jnp.exp(m_prev - m_new)
            m_sc[rows, :] = m_new
            p_sc[rows, :] = jnp.exp(s - jnp.concatenate([m_new] * rep, axis=1)).astype(BF16)
        acc_sc[...] = a_sc[...] * acc_sc[...] + jnp.dot(p_sc[...], values(t), preferred_element_type=F32)

    s_sc[0] = logits(0)

    def two_tiles(u, carry):
        t = 2 * u
        s_sc[1] = logits(t + 1)
        update(0, t, False)
        s_sc[0] = logits(t + 2)
        update(1, t + 1, False)
        return carry

    lax.fori_loop(0, n_full // 2, two_tiles, 0)

    @pl.when(n_full % 2 == 1)
    def _():
        s_sc[1] = logits(n_full)
        update(0, n_full - 1, False)
        update(1, n_full, True)

    @pl.when(n_full % 2 == 0)
    def _():
        update(0, n_full, True)


def _softmax_scratch(rows, tile):
    return [pltpu.VMEM((2, rows, tile), F32), pltpu.VMEM((rows, tile), BF16),
            pltpu.VMEM((rows, LANES), F32), pltpu.VMEM((rows, LANES), F32), pltpu.VMEM((rows, LANES), F32)]


def _normalized(acc):
    return acc[:, :HEAD_DIM] / acc[:, HEAD_DIM:HEAD_DIM + 1]


def _fox_kernel(q_ref, k_ref, v_ref, o_ref, s_sc, p_sc, m_sc, a_sc, acc_sc):
    tile = KEY_TILE
    i = pl.program_id(2)
    q = q_ref[...]

    def logits(t):
        k0 = pl.multiple_of(t * tile, tile)
        return lax.dot_general(q, k_ref[pl.ds(k0, tile), :], NT_DIMS, preferred_element_type=F32)

    def values(t):
        return v_ref[pl.ds(pl.multiple_of(t * tile, tile), tile), :]

    def causal(s, first_row):
        qi = first_row + lax.broadcasted_iota(jnp.int32, s.shape, 0)
        ki = lax.broadcasted_iota(jnp.int32, s.shape, 1)
        return jnp.where(ki <= qi, s, NEG_INF)

    _online_softmax_tiles(i, logits, values, causal, s_sc, p_sc, m_sc, a_sc, acc_sc)
    o_ref[...] = _normalized(acc_sc[...]).astype(o_ref.dtype)


def fox_attention(qkv):
    _, b, h, s, _ = qkv.shape
    tile = KEY_TILE
    assert s % tile == 0 and ATTN_ROWS == tile
    whole = lambda plane: pl.BlockSpec((None, None, None, s, LANES), lambda b_, h_, i: (plane, b_, h_, 0, 0))
    return pl.pallas_call(
        _fox_kernel,
        out_shape=jax.ShapeDtypeStruct((b, h, s, HEAD_DIM), BF16),
        grid=(b, h, s // tile),
        in_specs=[pl.BlockSpec((None, None, None, tile, LANES), lambda b_, h_, i: (0, b_, h_, i, 0)),
                  whole(1), whole(2)],
        out_specs=pl.BlockSpec((None, None, tile, HEAD_DIM), lambda b_, h_, i: (b_, h_, i, 0)),
        scratch_shapes=_softmax_scratch(ATTN_ROWS, tile),
        compiler_params=_cparams(3), name="fox_attention",
    )(qkv, qkv, qkv)


def _gelu_tanh(x):
    return 0.5 * x * (1.0 + jnp.tanh(np.sqrt(2.0 / np.pi).astype(np.float32) * (x + 0.044715 * (x * x * x))))


def _compress_kernel(xk_ref, xv_ref, pek_ref, pev_ref, w1k_ref, w2k_ref, w1v_ref, w2v_ref,
                     kcols_ref, onecol_ref, ko_ref, vo_ref):
    nc, half = xk_ref.shape

    def mlp(x_ref, pe_ref, w1_ref, w2_ref):
        x = x_ref[...]
        first = jnp.dot((x + pe_ref[:, :half]).astype(BF16), w1_ref[:half, :], preferred_element_type=F32)
        second = jnp.dot((x + pe_ref[:, half:]).astype(BF16), w1_ref[half:, :], preferred_element_type=F32)
        hid = _gelu_tanh(first + pltpu.roll(second, nc - 1, 0))
        return jnp.dot(hid.astype(BF16), w2_ref[...], preferred_element_type=F32)

    ko_ref[...] = _widen(mlp(xk_ref, pek_ref, w1k_ref, w2k_ref), LANES) + kcols_ref[...]
    vo_ref[...] = _widen(mlp(xv_ref, pev_ref, w1v_ref, w2v_ref), LANES) + onecol_ref[...]


def compress_blocks(chunks, pe_k, pe_v, w1_k, w2_k, w1_v, w2_v, key_cols, one_col):
    _, b, g, nc, half = chunks.shape
    feat = 2 * half
    chunk_spec = lambda plane: pl.BlockSpec((None, None, None, nc, half), lambda i, j: (plane, i, j, 0, 0))
    full = lambda shape: pl.BlockSpec(shape, lambda i, j: (0,) * len(shape))
    out = jax.ShapeDtypeStruct((b, g, nc, LANES), BF16)
    out_spec = pl.BlockSpec((None, None, nc, LANES), lambda i, j: (i, j, 0, 0))
    return pl.pallas_call(
        _compress_kernel,
        out_shape=(out, out),
        grid=(b, g),
        in_specs=[chunk_spec(0), chunk_spec(1), full((1, feat)), full((1, feat)),
                  full((feat, HEAD_DIM)), full((HEAD_DIM, HEAD_DIM)),
                  full((feat, HEAD_DIM)), full((HEAD_DIM, HEAD_DIM)),
                  full((nc, LANES)), full((1, LANES))],
        out_specs=(out_spec, out_spec),
        compiler_params=_cparams(2), name="nsa_compress",
    )(chunks, chunks, pe_k, pe_v, w1_k, w2_k, w1_v, w2_v, key_cols, one_col)


NSA_ROWS = NSA_HG * Q_BLOCK


def _select_blocks_t(score_ref, n_sel, n_top):
    groups = n_sel // SUBLANES
    grp = [score_ref[pl.ds(g * SUBLANES, SUBLANES), :] for g in range(groups)]
    blk = lax.broadcasted_iota(jnp.int32, (SUBLANES, LANES), 0)
    beaten_by = [jnp.zeros((SUBLANES, LANES), F32) for _ in range(groups)]
    for k in range(n_sel):
        sk = jnp.broadcast_to(score_ref[pl.ds(k, 1), :], (SUBLANES, LANES))
        for g in range(groups):
            if g * SUBLANES > k:
                wins = sk >= grp[g]
            elif g * SUBLANES + SUBLANES - 1 < k:
                wins = sk > grp[g]
            else:
                wins = (sk > grp[g]) | ((sk == grp[g]) & (blk + g * SUBLANES > k))
            beaten_by[g] = beaten_by[g] + jnp.where(wins, 1.0, 0.0)
    return jnp.concatenate([jnp.where(c < n_top, 1.0, 0.0) for c in beaten_by], axis=0)


def _nsa_kernel(q_ref, kc_ref, vc_ref, ks_ref, vs_ref, kw_ref, vw_ref, gl_ref, selmap_ref, o_ref,
                score_sc, qa_sc, s_sc, p_sc, m_sc, a_sc, acc_sc, *, seq, n_top):
    i = pl.program_id(2)
    q0 = i * Q_BLOCK
    n_cmp_pad = seq // CMP_STRIDE
    n_sel = seq // SEL_LEN
    q1 = q_ref[...]
    row = lax.broadcasted_iota(jnp.int32, (NSA_ROWS, 1), 0)
    qpos_col = q0 + (row & (Q_BLOCK - 1))

    cend = lax.broadcasted_iota(jnp.int32, (1, n_cmp_pad), 1) * CMP_STRIDE + (CMP_LEN - 1)
    valid_c = cend <= qpos_col
    lc = jnp.where(valid_c, lax.dot_general(q1, kc_ref[...], NT_DIMS, preferred_element_type=F32), NEG_INF)
    mc = jnp.max(lc, axis=-1, keepdims=True)
    pc = jnp.where(valid_c, jnp.exp(lc - mc), 0.0)
    sc = jnp.sum(pc, axis=-1, keepdims=True)
    pc = pc / jnp.where(sc > 0, sc, 1.0)
    o_c = jnp.dot(pc.astype(BF16), vc_ref[...], preferred_element_type=F32)[:, :HEAD_DIM]

    pcs = pc[0:Q_BLOCK]
    for hh in range(1, NSA_HG):
        pcs = pcs + pc[hh * Q_BLOCK:(hh + 1) * Q_BLOCK]
    hi = pcs.astype(BF16)
    lo = (pcs - hi.astype(F32)).astype(BF16)
    selmap = selmap_ref[...]
    imp_t = (lax.dot_general(selmap, hi, NT_DIMS, preferred_element_type=F32)
             + lax.dot_general(selmap, lo, NT_DIMS, preferred_element_type=F32))
    blk = lax.broadcasted_iota(jnp.int32, (n_sel, Q_BLOCK), 0)
    qrow = q0 + lax.broadcasted_iota(jnp.int32, (n_sel, Q_BLOCK), 1)
    qblk = qrow >> 6
    forced = (blk == 0) | (blk == qblk) | (blk == qblk - 1)
    score_sc[...] = jnp.where(blk * SEL_LEN <= qrow, imp_t + jnp.where(forced, FORCE_BONUS, 0.0), NEG_INF)
    sel_t = _select_blocks_t(score_sc, n_sel, n_top)

    sel_t = jnp.concatenate([sel_t, jnp.ones((LANES - n_sel, Q_BLOCK), F32)], axis=0).astype(BF16)
    eye = jnp.where(lax.broadcasted_iota(jnp.int32, (Q_BLOCK, Q_BLOCK), 0)
                    == lax.broadcasted_iota(jnp.int32, (Q_BLOCK, Q_BLOCK), 1), 1.0, 0.0).astype(BF16)
    sel = lax.dot_general(eye, sel_t, NT_DIMS, preferred_element_type=F32)
    closed = jnp.where(sel > 0.5, 0.0, NEG_INF).astype(BF16)
    qa_sc[:, 0:LANES] = q1
    for hh in range(NSA_HG):
        qa_sc[hh * Q_BLOCK:(hh + 1) * Q_BLOCK, LANES:2 * LANES] = closed

    band = WIN + Q_BLOCK
    w0 = pl.multiple_of(jnp.maximum(q0 - WIN, 0), Q_BLOCK)
    rel_w = (w0 + lax.broadcasted_iota(jnp.int32, (Q_BLOCK, band), 1)
             - (q0 + lax.broadcasted_iota(jnp.int32, (Q_BLOCK, band), 0)))
    bias_w = jnp.where((rel_w <= 0) & (rel_w > -WIN), 0.0, NEG_INF)
    sw = lax.dot_general(q1, kw_ref[pl.ds(w0, band), :], NT_DIMS, preferred_element_type=F32)
    sw = (sw.reshape(NSA_HG, Q_BLOCK, band) + bias_w[None]).reshape(NSA_ROWS, band)
    pw = jnp.exp(sw - jnp.max(sw, axis=-1, keepdims=True))
    o_w = _normalized(jnp.dot(pw.astype(BF16), vw_ref[pl.ds(w0, band), :], preferred_element_type=F32))
    gate = _sigmoid(gl_ref[...])
    out_cw = gate[:, 0:1] * o_c + gate[:, 2:3] * o_w

    tile = KEY_TILE
    n_full = q0 // tile

    def logits(t):
        k0 = pl.multiple_of(t * tile, tile)
        return lax.dot_general(qa_sc[...], ks_ref[pl.ds(k0, tile), :], NT_DIMS, preferred_element_type=F32)

    def values(t):
        return vs_ref[pl.ds(pl.multiple_of(t * tile, tile), tile), :]

    def causal(s, first_row):
        r = first_row + lax.broadcasted_iota(jnp.int32, s.shape, 0)
        kpos = n_full * tile + lax.broadcasted_iota(jnp.int32, s.shape, 1)
        return jnp.where(kpos <= q0 + (r & (Q_BLOCK - 1)), s, NEG_INF)

    _online_softmax_tiles(n_full, logits, values, causal, s_sc, p_sc, m_sc, a_sc, acc_sc)
    o_s = _normalized(acc_sc[...])
    o_ref[...] = (out_cw + gate[:, 1:2] * o_s).astype(o_ref.dtype)


def nsa_attention(q_aug, kc_aug, vc_aug, ks_aug, vs_aug, kw_aug, vw_aug, gate_logit, selmap):
    b, g, nb = q_aug.shape[:3]
    s = vs_aug.shape[2]
    nc = kc_aug.shape[2]
    n_sel = s // SEL_LEN
    assert s % KEY_TILE == 0 and s >= WIN + Q_BLOCK and n_sel <= LANES and NSA_ROWS == ATTN_ROWS
    per_block = lambda width: pl.BlockSpec((None, None, None, NSA_ROWS, width),
                                           lambda b_, g_, i: (b_, g_, i, 0, 0))
    per_group = lambda rows, width: pl.BlockSpec((None, None, rows, width), lambda b_, g_, i: (b_, g_, 0, 0))
    return pl.pallas_call(
        functools.partial(_nsa_kernel, seq=s, n_top=min(SEL_TOP, n_sel)),
        out_shape=jax.ShapeDtypeStruct((b, g, nb, NSA_ROWS, HEAD_DIM), BF16),
        grid=(b, g, nb),
        in_specs=[per_block(LANES), per_group(nc, LANES), per_group(nc, LANES),
                  per_group(s, 2 * LANES), per_group(s, LANES), per_group(s, LANES), per_group(s, LANES),
                  per_block(3), pl.BlockSpec((n_sel, nc), lambda b_, g_, i: (0, 0))],
        out_specs=per_block(HEAD_DIM),
        scratch_shapes=[pltpu.VMEM((n_sel, Q_BLOCK), F32), pltpu.VMEM((NSA_ROWS, 2 * LANES), BF16)]
        + _softmax_scratch(NSA_ROWS, KEY_TILE),
        compiler_params=_cparams(3), name="nsa_attention",
    )(q_aug, kc_aug, vc_aug, ks_aug, vs_aug, kw_aug, vw_aug, gate_logit, selmap)


def _causal_conv3(u, halo, w):
    row = lax.broadcasted_iota(jnp.int32, u.shape, 0)
    last = halo.shape[0] - 1
    prev1 = halo[last:last + 1]
    prev2 = halo[last - 1:last]
    u1 = jnp.where(row == 0, prev1, pltpu.roll(u, 1, 0))
    u2 = jnp.where(row == 0, prev2, jnp.where(row == 1, prev1, pltpu.roll(u, 2, 0)))
    return w[0:1] * u2 + w[1:2] * u1 + w[2:3] * u


MIX_ROWS = 512
MIX_COLS = 256


def _mixer_output_kernel(h_ref, wg_ref, bg_ref, cb_ref, cc_ref, cx_ref, cch_ref, cxh_ref, a_ref, c_ref,
                         wa_ref, wb_ref, wc_ref, cw_ref, wo_ref, x_ref, g_ref, xo_ref, ho_ref,
                         a_sc, b_sc, c_sc, mix_sc, *, tm, seq):
    d = x_ref.shape[1]
    for hh in range(FOX_HEADS):
        a_sc[:, hh * HEAD_DIM:(hh + 1) * HEAD_DIM] = a_ref[hh]
    for g in range(NSA_KV_GROUPS):
        for hg in range(NSA_HG):
            c0 = (g * NSA_HG + hg) * HEAD_DIM
            for qb in range(tm // Q_BLOCK):
                c_sc[qb * Q_BLOCK:(qb + 1) * Q_BLOCK, c0:c0 + HEAD_DIM] = c_ref[g, qb, hg * Q_BLOCK:(hg + 1) * Q_BLOCK, :]
    seq_start = (pl.program_id(0) * tm) % seq == 0
    u = cc_ref[...] * cx_ref[...]
    halo = jnp.where(seq_start, 0.0, cch_ref[...] * cxh_ref[...])
    b_sc[...] = (cb_ref[...] * _causal_conv3(u, halo, cw_ref[...])).astype(BF16)

    h = h_ref[...]
    for col in range(d // MIX_COLS):
        cols = slice(col * MIX_COLS, (col + 1) * MIX_COLS)
        mix = jnp.zeros((tm, MIX_COLS), F32)
        for branch, (src, w_ref) in enumerate(((a_sc, wa_ref), (b_sc, wb_ref), (c_sc, wc_ref))):
            gcols = slice(branch * d + col * MIX_COLS, branch * d + (col + 1) * MIX_COLS)
            gate = jnp.dot(h, wg_ref[:, gcols], preferred_element_type=F32) + bg_ref[:, gcols]
            mix = mix + _sigmoid(gate) * jnp.dot(src[...], w_ref[:, cols], preferred_element_type=F32)
        mix_sc[:, cols] = mix.astype(BF16)

    xn = x_ref[...] + jnp.dot(mix_sc[...], wo_ref[...], preferred_element_type=F32)
    xo_ref[...] = xn
    ho_ref[...] = _rms(xn, g_ref[...]).astype(ho_ref.dtype)


def mixer_output(h, w_gate, b_gate, zc, a_out, c_out, w_a, w_b, w_c, conv_w, w_o, x, g, seq):
    n, d = x.shape
    cw = w_b.shape[0]
    tm = min(MIX_ROWS, seq)
    tpb = seq // tm
    nb = tm // Q_BLOCK
    row = lambda width: pl.BlockSpec((tm, width), lambda i: (i, 0))
    ccol = lambda c: pl.BlockSpec((tm, cw), lambda i: (i, c))
    hcol = lambda c: pl.BlockSpec((SUBLANES, cw), lambda i: (jnp.maximum(i * (tm // SUBLANES) - 1, 0), c))
    once = lambda shape: pl.BlockSpec(shape, lambda i: (0, 0), pipeline_mode=pl.Buffered(1))
    return pl.pallas_call(
        functools.partial(_mixer_output_kernel, tm=tm, seq=seq),
        out_shape=(jax.ShapeDtypeStruct((n, d), F32), jax.ShapeDtypeStruct((n, d), BF16)),
        grid=(n // tm,),
        in_specs=[row(d), once(w_gate.shape), once((1, 3 * d)),
                  ccol(0), ccol(1), ccol(2), hcol(1), hcol(2),
                  pl.BlockSpec((None, FOX_HEADS, tm, HEAD_DIM), lambda i: (i // tpb, 0, i % tpb, 0)),
                  pl.BlockSpec((None, NSA_KV_GROUPS, nb, NSA_ROWS, HEAD_DIM), lambda i: (i // tpb, 0, i % tpb, 0, 0)),
                  once(w_a.shape), once(w_b.shape), once(w_c.shape), once(conv_w.shape), once(w_o.shape),
                  row(d), once((1, d))],
        out_specs=(row(d), row(d)),
        scratch_shapes=[pltpu.VMEM((tm, FOX_HEADS * HEAD_DIM), BF16), pltpu.VMEM((tm, cw), BF16),
                        pltpu.VMEM((tm, NSA_HEADS * HEAD_DIM), BF16), pltpu.VMEM((tm, d), BF16)],
        compiler_params=_cparams(1), name="mixer_output",
    )(h, w_gate, b_gate.reshape(1, -1), zc, zc, zc, zc, zc, a_out, c_out, w_a, w_b, w_c, conv_w, w_o,
      x, g.reshape(1, d))


def _ffn_up_kernel(h_ref, hh_ref, wu_ref, wv_ref, cw_ref, o_ref, *, tm, seq):
    seq_start = (pl.program_id(1) * tm) % seq == 0
    wu = wu_ref[...]
    u = jnp.dot(h_ref[...], wu, preferred_element_type=F32)
    v = jnp.dot(h_ref[...], wv_ref[...], preferred_element_type=F32)
    halo = jnp.where(seq_start, 0.0, jnp.dot(hh_ref[...], wu, preferred_element_type=F32))
    c = _causal_conv3(u, halo, cw_ref[...])
    o_ref[...] = (c * _sigmoid(c) * v).astype(o_ref.dtype)


def ffn_up(h, w_up, conv_w, seq, tm=512):
    n, d = h.shape
    f = conv_w.shape[1]
    tm = min(tm, n)
    nj = 2 if (f // 2) % LANES == 0 else 1
    tn = f // nj
    return pl.pallas_call(
        functools.partial(_ffn_up_kernel, tm=tm, seq=seq),
        out_shape=jax.ShapeDtypeStruct((n, f), BF16),
        grid=(nj, n // tm),
        in_specs=[pl.BlockSpec((tm, d), lambda j, i: (i, 0)),
                  pl.BlockSpec((2 * SUBLANES, d), lambda j, i: (jnp.maximum(i * (tm // (2 * SUBLANES)) - 1, 0), 0)),
                  pl.BlockSpec((d, tn), lambda j, i: (0, j)),
                  pl.BlockSpec((d, tn), lambda j, i: (0, nj + j)),
                  pl.BlockSpec((conv_w.shape[0], tn), lambda j, i: (0, j))],
        out_specs=pl.BlockSpec((tm, tn), lambda j, i: (i, j)),
        compiler_params=_cparams(2), name="ffn_up",
    )(h, h, w_up, w_up, conv_w)


def _selection_map(n_sel, n_cmp_pad):
    sel_start = np.arange(n_sel)[:, None] * SEL_LEN
    cmp_start = np.arange(n_cmp_pad)[None, :] * CMP_STRIDE
    ov = np.minimum(sel_start + SEL_LEN, cmp_start + CMP_LEN) - np.maximum(sel_start, cmp_start)
    ov = np.clip(ov, 0, None) / CMP_STRIDE
    ov[:, n_cmp_pad - 1] = 0.0
    return ov.astype(np.float32)


def _bf16_exact(a):
    a = np.asarray(a, np.float32)
    assert np.array_equal(a, a.astype(BF16).astype(np.float32)), "constant is not exact in bf16"
    return a


def _after_head(cols, width=LANES):
    out = np.zeros(cols.shape[:-1] + (width,), np.float32)
    out[..., HEAD_DIM:HEAD_DIM + cols.shape[-1]] = cols
    return out


def _alibi_query_cols(nqb):
    slope = np.array([2.0 ** (-8.0 * (i + 1) / NSA_HEADS) for i in range(NSA_HEADS)], np.float32)
    slope = slope.reshape(NSA_KV_GROUPS, 1, NSA_HG, 1)
    qpos = (np.arange(nqb)[:, None] * Q_BLOCK + np.arange(Q_BLOCK)[None, :]).reshape(1, nqb, 1, Q_BLOCK)
    cols = np.stack(np.broadcast_arrays(slope * SEL_LEN, slope, -slope * SEL_LEN * (qpos // SEL_LEN),
                                        -slope * (qpos % SEL_LEN)), axis=-1)
    return _bf16_exact(cols.reshape(NSA_KV_GROUPS, nqb, NSA_ROWS, 4))


def _alibi_key_cols(pos):
    pos = np.asarray(pos)
    return _bf16_exact(np.stack([pos // SEL_LEN, pos % SEL_LEN, np.ones_like(pos), np.ones_like(pos)], axis=-1))


def _split_w_in(w_in_l, b_in_l, d):
    sizes = (d, d, d, 512, 512, 512, FOX_HEADS, 512, 512, 512, 512, 128, 128, 128, 128, 128, 128, NSA_HEADS * 3)
    offs = np.concatenate([[0], np.cumsum(sizes)])
    col = lambda a, k: a[..., int(offs[k]):int(offs[k + 1])]
    (GA, GB, GC, FQ, FK, FV, FF, CB, CC, CX, NQ, NKC, NVC, NKS, NVS, NKW, NVW, NG) = range(18)
    pad = LANES - (FOX_HEADS + NSA_HEADS * 3)

    prescale = {FQ: SCALE, NQ: SCALE}

    def group(idx, extra=0):
        w = jnp.concatenate([col(w_in_l, k) * prescale.get(k, 1.0) for k in idx], axis=-1)
        bias = jnp.concatenate([col(b_in_l, k) * prescale.get(k, 1.0) for k in idx], axis=-1)
        if extra:
            w = jnp.pad(w, ((0, 0), (0, extra)))
            bias = jnp.pad(bias, ((0, extra),))
        return w.astype(BF16), bias

    return (group((FQ, FK, FV)), group((NQ, NKS, NVS, NKW, NVW, NKC, NVC)),
            group((GA, GB, GC)), group((CB, CC, CX)), group((FF, NG), pad))


def kernel(x, norm_mix_g, w_in, b_in, cmp_pe_k, cmp_w1_k, cmp_w2_k, cmp_pe_v, cmp_w1_v, cmp_w2_v, sc_conv_w, w_br_a, w_br_b, w_br_c, w_o, norm_ffn_g, w_up, ffn_conv_w, w_down, norm_final_g):
    b, s, d = x.shape
    depth = w_in.shape[0]
    n = b * s
    nqb = s // Q_BLOCK
    n_cmp_pad = s // CMP_STRIDE
    const = lambda a: jnp.asarray(a, dtype=BF16)
    selmap = const(_selection_map(s // SEL_LEN, n_cmp_pad))
    one_col = const(_after_head(np.ones((1, 1), np.float32)))
    q_cols = const(_after_head(_alibi_query_cols(nqb)))
    cmp_cols = const(_after_head(_alibi_key_cols(np.arange(n_cmp_pad) * CMP_STRIDE + CMP_LEN - 1)))
    key_cols = _after_head(_alibi_key_cols(np.arange(s)))
    key_block = (np.arange(s)[:, None] // SEL_LEN == np.arange(LANES)[None, :]).astype(np.float32)
    sel_cols = const(np.concatenate([key_cols, key_block], axis=-1))
    key_cols = const(key_cols)

    xr = x.reshape(n, d)
    h = rmsnorm_rows(xr, norm_mix_g[0], BF16)
    for l in range(depth):
        ((w_fox, b_fox), (w_nsa, b_nsa), (w_gate, b_gate), (w_conv, b_conv),
         (w_sm, b_sm)) = _split_w_in(w_in[l], b_in[l], d)
        zc = matmul_bias(h, w_conv, b_conv, F32, tn=w_conv.shape[1])
        zs3 = matmul_bias(h, w_sm, b_sm, F32, tn=LANES).reshape(b, s, -1)

        bias_cols = fox_bias_columns(zs3[..., 0:FOX_HEADS].transpose(0, 2, 1))
        a_out = fox_attention(fox_qkv_projection(h, w_fox, b_fox, bias_cols, b, s))

        q_aug, ks_aug, vs_aug, kw_aug, vw_aug, cmp_in = nsa_projection(
            h, w_nsa, b_nsa, q_cols, sel_cols, key_cols, one_col, b, s)
        kc_aug, vc_aug = compress_blocks(
            cmp_in.reshape(2, b, NSA_KV_GROUPS, n_cmp_pad, CMP_STRIDE * HEAD_DIM),
            cmp_pe_k[l].reshape(1, -1), cmp_pe_v[l].reshape(1, -1),
            cmp_w1_k[l].astype(BF16), cmp_w2_k[l].astype(BF16),
            cmp_w1_v[l].astype(BF16), cmp_w2_v[l].astype(BF16), cmp_cols, one_col)
        ng = zs3[..., FOX_HEADS:FOX_HEADS + NSA_HEADS * 3].reshape(b, nqb, Q_BLOCK, NSA_KV_GROUPS, NSA_HG, 3)
        ng = ng.transpose(0, 3, 1, 4, 2, 5).reshape(b, NSA_KV_GROUPS, nqb, NSA_ROWS, 3)
        c_out = nsa_attention(q_aug, kc_aug, vc_aug, ks_aug, vs_aug, kw_aug, vw_aug, ng, selmap)

        xr, h2 = mixer_output(h, w_gate, b_gate, zc, a_out, c_out, w_br_a[l].astype(BF16),
                              w_br_b[l].astype(BF16), w_br_c[l].astype(BF16), sc_conv_w[l],
                              w_o[l].astype(BF16), xr, norm_ffn_g[l], s)

        act = ffn_up(h2, w_up[l].astype(BF16), ffn_conv_w[l], s)
        if l + 1 < depth:
            xr, h = residual_matmul_norm(act, w_down[l].astype(BF16), xr, norm_mix_g[l + 1], BF16)
        else:
            out = residual_matmul_norm(act, w_down[l].astype(BF16), xr, norm_final_g, F32,
                                       keep_residual=False)
    return out.reshape(b, s, d)
```

```python
import functools

import numpy as np
import jax
import jax.numpy as jnp
from jax import lax
from jax.experimental import pallas as pl
from jax.experimental.pallas import tpu as pltpu

F32 = jnp.float32
BF16 = jnp.bfloat16

HEAD_DIM = 64
FOX_HEADS = 8
NSA_HEADS = 8
NSA_KV_GROUPS = 2
NSA_HG = NSA_HEADS // NSA_KV_GROUPS
CMP_LEN = 32
CMP_STRIDE = 16
SEL_LEN = 64
SEL_TOP = 16
WIN = 512
Q_BLOCK = 128
EPS = 1e-6
NEG_INF = -1e30
FORCE_BONUS = 1e4
SCALE = HEAD_DIM ** -0.5

LANES = 128
SUBLANES = 8
VMEM_LIMIT = 48 * 1024 * 1024
PROJ_ROWS = 1024

NT_DIMS = (((1,), (1,)), ((), ()))


def _cparams(n_parallel, n_arbitrary=0):
    return pltpu.CompilerParams(
        dimension_semantics=("parallel",) * n_parallel + ("arbitrary",) * n_arbitrary,
        vmem_limit_bytes=VMEM_LIMIT)


def _sigmoid(x):
    return 1.0 / (1.0 + jnp.exp(-x))


def _rms(x, g):
    ms = jnp.mean(x * x, axis=-1, keepdims=True)
    return x * lax.rsqrt(ms + EPS) * g


def _widen(x, width):
    return jnp.concatenate([x, jnp.zeros((x.shape[0], width - x.shape[1]), x.dtype)], axis=1).astype(BF16)


def _norm_kernel(x_ref, g_ref, o_ref):
    o_ref[...] = _rms(x_ref[...], g_ref[...]).astype(o_ref.dtype)


def rmsnorm_rows(x, g, out_dtype, tm=1024):
    n, d = x.shape
    tm = min(tm, n)
    return pl.pallas_call(
        _norm_kernel,
        out_shape=jax.ShapeDtypeStruct((n, d), out_dtype),
        grid=(n // tm,),
        in_specs=[pl.BlockSpec((tm, d), lambda i: (i, 0)),
                  pl.BlockSpec((1, d), lambda i: (0, 0))],
        out_specs=pl.BlockSpec((tm, d), lambda i: (i, 0)),
        compiler_params=_cparams(1),
        name="rmsnorm",
    )(x, g.reshape(1, d))


def _mm_bias_kernel(a_ref, w_ref, b_ref, o_ref):
    acc = jnp.dot(a_ref[...], w_ref[...], preferred_element_type=F32)
    o_ref[...] = (acc + b_ref[...]).astype(o_ref.dtype)


def matmul_bias(a, w, b, out_dtype, tn, tm=PROJ_ROWS):
    n, k = a.shape
    m = w.shape[1]
    tm = min(tm, n)
    return pl.pallas_call(
        _mm_bias_kernel,
        out_shape=jax.ShapeDtypeStruct((n, m), out_dtype),
        grid=(n // tm, m // tn),
        in_specs=[pl.BlockSpec((tm, k), lambda i, j: (i, 0)),
                  pl.BlockSpec((k, tn), lambda i, j: (0, j)),
                  pl.BlockSpec((1, tn), lambda i, j: (0, j))],
        out_specs=pl.BlockSpec((tm, tn), lambda i, j: (i, j)),
        compiler_params=_cparams(2),
        name="matmul_bias",
    )(a, w, b.reshape(1, m))


HEAD_PROJ_ROWS = 512


def _fox_qkv_kernel(a_ref, w_ref, b_ref, aug_ref, o_ref):
    acc = jnp.dot(a_ref[...], w_ref[...], preferred_element_type=F32) + b_ref[...]
    lane = lax.broadcasted_iota(jnp.int32, (acc.shape[0], LANES), 1)
    ones_col = jnp.where(lane == HEAD_DIM, 1.0, 0.0).astype(BF16)
    for plane in range(3):
        for hh in range(FOX_HEADS):
            c0 = (plane * FOX_HEADS + hh) * HEAD_DIM
            extra = ones_col if plane == 2 else aug_ref[plane, hh]
            o_ref[plane, hh] = _widen(acc[:, c0:c0 + HEAD_DIM], LANES) + extra


def fox_qkv_projection(h, w, b, aug, batch, seq):
    n, d = h.shape
    tm = min(HEAD_PROJ_ROWS, seq)
    tpb = seq // tm
    plane_block = lambda planes: pl.BlockSpec((planes, None, FOX_HEADS, tm, LANES),
                                              lambda i: (0, i // tpb, 0, i % tpb, 0))
    return pl.pallas_call(
        _fox_qkv_kernel,
        out_shape=jax.ShapeDtypeStruct((3, batch, FOX_HEADS, seq, LANES), BF16),
        grid=(n // tm,),
        in_specs=[pl.BlockSpec((tm, d), lambda i: (i, 0)),
                  pl.BlockSpec(w.shape, lambda i: (0, 0)),
                  pl.BlockSpec((1, w.shape[1]), lambda i: (0, 0)),
                  plane_block(2)],
        out_specs=plane_block(3),
        compiler_params=_cparams(1), name="fox_qkv_projection",
    )(h, w, b.reshape(1, -1), aug)


def _nsa_proj_kernel(a_ref, w_ref, b_ref, qcols_ref, selcols_ref, keycols_ref, onecol_ref,
                     q_ref, ks_ref, vs_ref, kw_ref, vw_ref, cmp_ref):
    acc = jnp.dot(a_ref[...], w_ref[...], preferred_element_type=F32) + b_ref[...]
    for g in range(NSA_KV_GROUPS):
        for hg in range(NSA_HG):
            c0 = (g * NSA_HG + hg) * HEAD_DIM
            for qb in range(acc.shape[0] // Q_BLOCK):
                rows = slice(hg * Q_BLOCK, (hg + 1) * Q_BLOCK)
                q_ref[g, qb, rows, :] = (_widen(acc[qb * Q_BLOCK:(qb + 1) * Q_BLOCK, c0:c0 + HEAD_DIM], LANES)
                                         + qcols_ref[g, qb, rows, :])
    gw = NSA_KV_GROUPS * HEAD_DIM
    kv0 = NSA_HEADS * HEAD_DIM
    for g in range(NSA_KV_GROUPS):
        part = lambda idx: acc[:, kv0 + idx * gw + g * HEAD_DIM: kv0 + idx * gw + (g + 1) * HEAD_DIM]
        ks_ref[g] = _widen(part(0), 2 * LANES) + selcols_ref[...]
        vs_ref[g] = _widen(part(1), LANES) + onecol_ref[...]
        kw_ref[g] = _widen(part(2), LANES) + keycols_ref[...]
        vw_ref[g] = _widen(part(3), LANES) + onecol_ref[...]
        cmp_ref[0, g] = part(4)
        cmp_ref[1, g] = part(5)


def nsa_projection(h, w, b, q_cols, sel_cols, key_cols, one_col, batch, seq):
    n, d = h.shape
    tm = min(HEAD_PROJ_ROWS, seq)
    tpb = seq // tm
    nb = tm // Q_BLOCK
    rows = NSA_HG * Q_BLOCK
    out = lambda width: jax.ShapeDtypeStruct((batch, NSA_KV_GROUPS, seq, width), BF16)
    out_spec = lambda width: pl.BlockSpec((None, NSA_KV_GROUPS, tm, width), lambda i: (i // tpb, 0, i % tpb, 0))
    return pl.pallas_call(
        _nsa_proj_kernel,
        out_shape=(jax.ShapeDtypeStruct((batch, NSA_KV_GROUPS, seq // Q_BLOCK, rows, LANES), BF16),
                   out(2 * LANES), out(LANES), out(LANES), out(LANES),
                   jax.ShapeDtypeStruct((2, batch, NSA_KV_GROUPS, seq, HEAD_DIM), F32)),
        grid=(n // tm,),
        in_specs=[pl.BlockSpec((tm, d), lambda i: (i, 0)),
                  pl.BlockSpec(w.shape, lambda i: (0, 0)),
                  pl.BlockSpec((1, w.shape[1]), lambda i: (0, 0)),
                  pl.BlockSpec((NSA_KV_GROUPS, nb, rows, LANES), lambda i: (0, i % tpb, 0, 0)),
                  pl.BlockSpec((tm, 2 * LANES), lambda i: (i % tpb, 0)),
                  pl.BlockSpec((tm, LANES), lambda i: (i % tpb, 0)),
                  pl.BlockSpec((1, LANES), lambda i: (0, 0))],
        out_specs=(pl.BlockSpec((None, NSA_KV_GROUPS, nb, rows, LANES), lambda i: (i // tpb, 0, i % tpb, 0, 0)),
                   out_spec(2 * LANES), out_spec(LANES), out_spec(LANES), out_spec(LANES),
                   pl.BlockSpec((2, None, NSA_KV_GROUPS, tm, HEAD_DIM), lambda i: (0, i // tpb, 0, i % tpb, 0))),
        compiler_params=_cparams(1), name="nsa_projection",
    )(h, w, b.reshape(1, -1), q_cols, sel_cols, key_cols, one_col)


def _res_norm_kernel(a_ref, w_ref, x_ref, g_ref, xo_ref, ho_ref):
    xn = x_ref[...] + jnp.dot(a_ref[...], w_ref[...], preferred_element_type=F32)
    xo_ref[...] = xn
    ho_ref[...] = _rms(xn, g_ref[...]).astype(ho_ref.dtype)


def _res_norm_only_kernel(a_ref, w_ref, x_ref, g_ref, ho_ref):
    xn = x_ref[...] + jnp.dot(a_ref[...], w_ref[...], preferred_element_type=F32)
    ho_ref[...] = _rms(xn, g_ref[...]).astype(ho_ref.dtype)


def residual_matmul_norm(a, w, x, g, norm_dtype, keep_residual=True, tm=512):
    n, k = a.shape
    d = w.shape[1]
    tm = min(tm, n)
    row = lambda i: (i, 0)
    in_specs = [pl.BlockSpec((tm, k), row),
                pl.BlockSpec((k, d), lambda i: (0, 0)),
                pl.BlockSpec((tm, d), row),
                pl.BlockSpec((1, d), lambda i: (0, 0))]
    if keep_residual:
        return pl.pallas_call(
            _res_norm_kernel,
            out_shape=(jax.ShapeDtypeStruct((n, d), F32), jax.ShapeDtypeStruct((n, d), norm_dtype)),
            grid=(n // tm,), in_specs=in_specs,
            out_specs=(pl.BlockSpec((tm, d), row), pl.BlockSpec((tm, d), row)),
            compiler_params=_cparams(1), name="residual_matmul_norm",
        )(a, w, x, g.reshape(1, d))
    return pl.pallas_call(
        _res_norm_only_kernel,
        out_shape=jax.ShapeDtypeStruct((n, d), norm_dtype),
        grid=(n // tm,), in_specs=in_specs,
        out_specs=pl.BlockSpec((tm, d), row),
        compiler_params=_cparams(1), name="residual_matmul_final_norm",
    )(a, w, x, g.reshape(1, d))


CUM_BLOCK = 256


def _bf16_pieces(x):
    hi = x.astype(BF16).astype(F32)
    r1 = x - hi
    mid = r1.astype(BF16).astype(F32)
    lo = (r1 - mid).astype(BF16).astype(F32)
    return hi, mid, lo


CUM_CHUNK = 1024


def _fox_bias_columns_kernel(f_ref, o_ref, carry_sc):
    heads, chunk = f_ref.shape

    @pl.when(pl.program_id(1) == 0)
    def _():
        carry_sc[...] = jnp.zeros_like(carry_sc)

    x = f_ref[...]
    ls = jnp.minimum(x, 0.0) - jnp.log1p(jnp.exp(-jnp.abs(x)))
    r = lax.broadcasted_iota(jnp.int32, (CUM_BLOCK, CUM_BLOCK), 0)
    c = lax.broadcasted_iota(jnp.int32, (CUM_BLOCK, CUM_BLOCK), 1)
    tri = jnp.where(r <= c, 1.0, 0.0).astype(BF16)
    eye = jnp.where(r == c, 1.0, 0.0).astype(BF16)
    lane_q = lax.broadcasted_iota(jnp.int32, (2 * LANES, CUM_BLOCK), 0) - HEAD_DIM
    lane_k = lane_q - LANES
    ones_rows = jnp.where(((lane_q >= 3) & (lane_q < 6)) | ((lane_k >= 0) & (lane_k < 3)), 1.0, 0.0)
    carry = carry_sc[...]
    for blk in range(chunk // CUM_BLOCK):
        cols = slice(blk * CUM_BLOCK, (blk + 1) * CUM_BLOCK)
        cs = carry
        for piece in _bf16_pieces(ls[:, cols]):
            cs = cs + jnp.dot(piece.astype(BF16), tri, preferred_element_type=F32)
        carry = cs[:, CUM_BLOCK - 1:CUM_BLOCK]
        pieces = _bf16_pieces(cs)
        for hh in range(heads):
            both = ones_rows
            for k, piece in enumerate(pieces):
                both = jnp.where(lane_q == k, piece[hh:hh + 1], both)
                both = jnp.where(lane_k == 3 + k, -piece[hh:hh + 1], both)
            cols_t = lax.dot_general(eye, both.astype(BF16), NT_DIMS, preferred_element_type=F32)
            o_ref[0, hh, cols, :] = cols_t[:, :LANES].astype(o_ref.dtype)
            o_ref[1, hh, cols, :] = cols_t[:, LANES:].astype(o_ref.dtype)
    carry_sc[...] = carry


def fox_bias_columns(f_t):
    b, h, s = f_t.shape
    chunk = min(CUM_CHUNK, s)
    return pl.pallas_call(
        _fox_bias_columns_kernel,
        out_shape=jax.ShapeDtypeStruct((2, b, h, s, LANES), BF16),
        grid=(b, s // chunk),
        in_specs=[pl.BlockSpec((None, h, chunk), lambda i, j: (i, 0, j))],
        out_specs=pl.BlockSpec((2, None, h, chunk, LANES), lambda i, j: (0, i, 0, j, 0)),
        scratch_shapes=[pltpu.VMEM((h, 1), F32)],
        compiler_params=_cparams(1, 1), name="fox_bias_columns",
    )(f_t)


KEY_TILE = 512
SOFTMAX_ROWS = 64


def _online_softmax_tiles(n_full, logits, values, last_bias, s_sc, p_sc, m_sc, a_sc, acc_sc):
    rows_total, tile = p_sc.shape
    rep = tile // LANES
    m_sc[...] = jnp.full_like(m_sc, NEG_INF)
    acc_sc[...] = jnp.zeros_like(acc_sc)

    def update(slot, t, last):
        for c in range(rows_total // SOFTMAX_ROWS):
            rows = pl.ds(c * SOFTMAX_ROWS, SOFTMAX_ROWS)
            s = s_sc[slot, rows, :]
            if last:
                s = last_bias(s, c * SOFTMAX_ROWS)
            m_prev = m_sc[rows, :]
            m_new = jnp.maximum(m_prev, jnp.max(s, axis=-1, keepdims=True))
            a_sc[rows, :] = jnp.exp(m_prev - m_new)
            m_sc[rows, :] = m_new
            p_sc[rows, :] = jnp.exp(s - jnp.concatenate([m_new] * rep, axis=1)).astype(BF16)
        acc_sc[...] = a_sc[...] * acc_sc[...] + jnp.dot(p_sc[...], values(t), preferred_element_type=F32)

    s_sc[0] = logits(0)

    def two_tiles(u, carry):
        t = 2 * u
        s_sc[1] = logits(t + 1)
        update(0, t, False)
        s_sc[0] = logits(t + 2)
        update(1, t + 1, False)
        return carry

    lax.fori_loop(0, n_full // 2, two_tiles, 0)

    @pl.when(n_full % 2 == 1)
    def _():
        s_sc[1] = logits(n_full)
        update(0, n_full - 1, False)
        update(1, n_full, True)

    @pl.when(n_full % 2 == 0)
    def _():
        update(0, n_full, True)


def _softmax_scratch(rows, tile):
    return [pltpu.VMEM((2, rows, tile), F32), pltpu.VMEM((rows, tile), BF16),
            pltpu.VMEM((rows, LANES), F32), pltpu.VMEM((rows, LANES), F32), pltpu.VMEM((rows, LANES), F32)]


def _normalized(acc):
    return acc[:, :HEAD_DIM] / acc[:, HEAD_DIM:HEAD_DIM + 1]


def _fox_kernel(q_ref, k_ref, v_ref, o_ref, s_sc, p_sc, m_sc, a_sc, acc_sc):
    tile = KEY_TILE
    i = pl.program_id(2)
    q = q_ref[...]

    def logits(t):
        k0 = pl.multiple_of(t * tile, tile)
        return lax.dot_general(q, k_ref[pl.ds(k0, tile), :], NT_DIMS, preferred_element_type=F32)

    def values(t):
        return v_ref[pl.ds(pl.multiple_of(t * tile, tile), tile), :]

    def causal(s, first_row):
        qi = first_row + lax.broadcasted_iota(jnp.int32, s.shape, 0)
        ki = lax.broadcasted_iota(jnp.int32, s.shape, 1)
        return jnp.where(ki <= qi, s, NEG_INF)

    _online_softmax_tiles(i, logits, values, causal, s_sc, p_sc, m_sc, a_sc, acc_sc)
    o_ref[...] = _normalized(acc_sc[...]).astype(o_ref.dtype)


def fox_attention(qkv):
    _, b, h, s, _ = qkv.shape
    tile = KEY_TILE
    assert s % tile == 0
    whole = lambda plane: pl.BlockSpec((None, None, None, s, LANES), lambda b_, h_, i: (plane, b_, h_, 0, 0))
    return pl.pallas_call(
        _fox_kernel,
        out_shape=jax.ShapeDtypeStruct((b, h, s, HEAD_DIM), BF16),
        grid=(b, h, s // tile),
        in_specs=[pl.BlockSpec((None, None, None, tile, LANES), lambda b_, h_, i: (0, b_, h_, i, 0)),
                  whole(1), whole(2)],
        out_specs=pl.BlockSpec((None, None, tile, HEAD_DIM), lambda b_, h_, i: (b_, h_, i, 0)),
        scratch_shapes=_softmax_scratch(tile, tile),
        compiler_params=_cparams(3), name="fox_attention",
    )(qkv, qkv, qkv)


def _gelu_tanh(x):
    return 0.5 * x * (1.0 + jnp.tanh(np.sqrt(2.0 / np.pi).astype(np.float32) * (x + 0.044715 * (x * x * x))))


def _compress_kernel(xk_ref, xv_ref, pek_ref, pev_ref, w1k_ref, w2k_ref, w1v_ref, w2v_ref,
                     kcols_ref, onecol_ref, ko_ref, vo_ref):
    nc, half = xk_ref.shape

    def mlp(x_ref, pe_ref, w1_ref, w2_ref):
        x = x_ref[...]
        first = jnp.dot((x + pe_ref[:, :half]).astype(BF16), w1_ref[:half, :], preferred_element_type=F32)
        second = jnp.dot((x + pe_ref[:, half:]).astype(BF16), w1_ref[half:, :], preferred_element_type=F32)
        hid = _gelu_tanh(first + pltpu.roll(second, nc - 1, 0))
        return jnp.dot(hid.astype(BF16), w2_ref[...], preferred_element_type=F32)

    ko_ref[...] = _widen(mlp(xk_ref, pek_ref, w1k_ref, w2k_ref), LANES) + kcols_ref[...]
    vo_ref[...] = _widen(mlp(xv_ref, pev_ref, w1v_ref, w2v_ref), LANES) + onecol_ref[...]


def compress_blocks(chunks, pe_k, pe_v, w1_k, w2_k, w1_v, w2_v, key_cols, one_col):
    _, b, g, nc, half = chunks.shape
    feat = 2 * half
    chunk_spec = lambda plane: pl.BlockSpec((None, None, None, nc, half), lambda i, j: (plane, i, j, 0, 0))
    full = lambda shape: pl.BlockSpec(shape, lambda i, j: (0,) * len(shape))
    out = jax.ShapeDtypeStruct((b, g, nc, LANES), BF16)
    out_spec = pl.BlockSpec((None, None, nc, LANES), lambda i, j: (i, j, 0, 0))
    return pl.pallas_call(
        _compress_kernel,
        out_shape=(out, out),
        grid=(b, g),
        in_specs=[chunk_spec(0), chunk_spec(1), full((1, feat)), full((1, feat)),
                  full((feat, HEAD_DIM)), full((HEAD_DIM, HEAD_DIM)),
                  full((feat, HEAD_DIM)), full((HEAD_DIM, HEAD_DIM)),
                  full((nc, LANES)), full((1, LANES))],
        out_specs=(out_spec, out_spec),
        compiler_params=_cparams(2), name="nsa_compress",
    )(chunks, chunks, pe_k, pe_v, w1_k, w2_k, w1_v, w2_v, key_cols, one_col)


NSA_ROWS = NSA_HG * Q_BLOCK
NSA_QB = 2
STEP_Q = NSA_QB * Q_BLOCK
STEP_ROWS = NSA_QB * NSA_ROWS


def _select_blocks_t(score_ref, n_sel, n_top):
    nq = score_ref.shape[1]
    groups = n_sel // SUBLANES
    grp = [score_ref[pl.ds(g * SUBLANES, SUBLANES), :] for g in range(groups)]
    blk = lax.broadcasted_iota(jnp.int32, (SUBLANES, nq), 0)
    beaten_by = [jnp.zeros((SUBLANES, nq), F32) for _ in range(groups)]
    for k in range(n_sel):
        sk = jnp.broadcast_to(score_ref[pl.ds(k, 1), :], (SUBLANES, nq))
        for g in range(groups):
            if g * SUBLANES > k:
                wins = sk >= grp[g]
            elif g * SUBLANES + SUBLANES - 1 < k:
                wins = sk > grp[g]
            else:
                wins = (sk > grp[g]) | ((sk == grp[g]) & (blk + g * SUBLANES > k))
            beaten_by[g] = beaten_by[g] + jnp.where(wins, 1.0, 0.0)
    return jnp.concatenate([jnp.where(c < n_top, 1.0, 0.0) for c in beaten_by], axis=0)


def _nsa_kernel(q_ref, kc_ref, vc_ref, ks_ref, vs_ref, kw_ref, vw_ref, gl_ref, selmap_ref, o_ref,
                score_sc, qa_sc, s_sc, p_sc, m_sc, a_sc, acc_sc, *, seq, n_top):
    i = pl.program_id(2)
    q0 = i * STEP_Q
    n_cmp_pad = seq // CMP_STRIDE
    n_sel = seq // SEL_LEN
    q1 = q_ref[...].reshape(STEP_ROWS, LANES)
    row = lax.broadcasted_iota(jnp.int32, (STEP_ROWS, 1), 0)

    def query_pos(r):
        return q0 + (r // NSA_ROWS) * Q_BLOCK + (r & (Q_BLOCK - 1))

    qpos_col = query_pos(row)

    cend = lax.broadcasted_iota(jnp.int32, (1, n_cmp_pad), 1) * CMP_STRIDE + (CMP_LEN - 1)
    valid_c = cend <= qpos_col
    lc = jnp.where(valid_c, lax.dot_general(q1, kc_ref[...], NT_DIMS, preferred_element_type=F32), NEG_INF)
    mc = jnp.max(lc, axis=-1, keepdims=True)
    pc = jnp.where(valid_c, jnp.exp(lc - mc), 0.0)
    sc = jnp.sum(pc, axis=-1, keepdims=True)
    pc = pc / jnp.where(sc > 0, sc, 1.0)
    o_c = jnp.dot(pc.astype(BF16), vc_ref[...], preferred_element_type=F32)[:, :HEAD_DIM]

    per_block = []
    for qb in range(NSA_QB):
        base = qb * NSA_ROWS
        acc_q = pc[base:base + Q_BLOCK]
        for hh in range(1, NSA_HG):
            acc_q = acc_q + pc[base + hh * Q_BLOCK:base + (hh + 1) * Q_BLOCK]
        per_block.append(acc_q)
    pcs = jnp.concatenate(per_block, axis=0)
    hi = pcs.astype(BF16)
    lo = (pcs - hi.astype(F32)).astype(BF16)
    selmap = selmap_ref[...]
    imp_t = (lax.dot_general(selmap, hi, NT_DIMS, preferred_element_type=F32)
             + lax.dot_general(selmap, lo, NT_DIMS, preferred_element_type=F32))
    blk = lax.broadcasted_iota(jnp.int32, (n_sel, STEP_Q), 0)
    qrow = q0 + lax.broadcasted_iota(jnp.int32, (n_sel, STEP_Q), 1)
    qblk = qrow >> 6
    forced = (blk == 0) | (blk == qblk) | (blk == qblk - 1)
    score_sc[...] = jnp.where(blk * SEL_LEN <= qrow, imp_t + jnp.where(forced, FORCE_BONUS, 0.0), NEG_INF)
    sel_t = _select_blocks_t(score_sc, n_sel, n_top)

    sel_t = jnp.concatenate([sel_t, jnp.ones((LANES - n_sel, STEP_Q), F32)], axis=0).astype(BF16)
    eye = jnp.where(lax.broadcasted_iota(jnp.int32, (STEP_Q, STEP_Q), 0)
                    == lax.broadcasted_iota(jnp.int32, (STEP_Q, STEP_Q), 1), 1.0, 0.0).astype(BF16)
    sel = lax.dot_general(eye, sel_t, NT_DIMS, preferred_element_type=F32)
    closed = jnp.where(sel > 0.5, 0.0, NEG_INF).astype(BF16)
    qa_sc[:, 0:LANES] = q1
    for qb in range(NSA_QB):
        for hh in range(NSA_HG):
            r0 = qb * NSA_ROWS + hh * Q_BLOCK
            qa_sc[r0:r0 + Q_BLOCK, LANES:2 * LANES] = closed[qb * Q_BLOCK:(qb + 1) * Q_BLOCK]

    band = WIN + STEP_Q
    w0 = pl.multiple_of(jnp.maximum(q0 - WIN, 0), Q_BLOCK)
    rel_w = (w0 + lax.broadcasted_iota(jnp.int32, (STEP_Q, band), 1)
             - (q0 + lax.broadcasted_iota(jnp.int32, (STEP_Q, band), 0)))
    bias_w = jnp.where((rel_w <= 0) & (rel_w > -WIN), 0.0, NEG_INF).reshape(NSA_QB, 1, Q_BLOCK, band)
    sw = lax.dot_general(q1, kw_ref[pl.ds(w0, band), :], NT_DIMS, preferred_element_type=F32)
    sw = (sw.reshape(NSA_QB, NSA_HG, Q_BLOCK, band) + bias_w).reshape(STEP_ROWS, band)
    pw = jnp.exp(sw - jnp.max(sw, axis=-1, keepdims=True))
    o_w = _normalized(jnp.dot(pw.astype(BF16), vw_ref[pl.ds(w0, band), :], preferred_element_type=F32))
    gate = _sigmoid(gl_ref[...].reshape(STEP_ROWS, 3))
    out_cw = gate[:, 0:1] * o_c + gate[:, 2:3] * o_w

    tile = KEY_TILE
    n_full = q0 // tile

    def logits(t):
        k0 = pl.multiple_of(t * tile, tile)
        return lax.dot_general(qa_sc[...], ks_ref[pl.ds(k0, tile), :], NT_DIMS, preferred_element_type=F32)

    def values(t):
        return vs_ref[pl.ds(pl.multiple_of(t * tile, tile), tile), :]

    def causal(s, first_row):
        r = first_row + lax.broadcasted_iota(jnp.int32, s.shape, 0)
        kpos = n_full * tile + lax.broadcasted_iota(jnp.int32, s.shape, 1)
        return jnp.where(kpos <= query_pos(r), s, NEG_INF)

    _online_softmax_tiles(n_full, logits, values, causal, s_sc, p_sc, m_sc, a_sc, acc_sc)
    o_s = _normalized(acc_sc[...])
    o_ref[...] = (out_cw + gate[:, 1:2] * o_s).astype(o_ref.dtype).reshape(NSA_QB, NSA_ROWS, HEAD_DIM)


def nsa_attention(q_aug, kc_aug, vc_aug, ks_aug, vs_aug, kw_aug, vw_aug, gate_logit, selmap):
    b, g, nb = q_aug.shape[:3]
    s = vs_aug.shape[2]
    nc = kc_aug.shape[2]
    n_sel = s // SEL_LEN
    assert s % KEY_TILE == 0 and KEY_TILE % STEP_Q == 0 and s >= WIN + STEP_Q and n_sel <= LANES
    per_step = lambda width: pl.BlockSpec((None, None, NSA_QB, NSA_ROWS, width),
                                          lambda b_, g_, i: (b_, g_, i, 0, 0))
    per_group = lambda rows, width: pl.BlockSpec((None, None, rows, width), lambda b_, g_, i: (b_, g_, 0, 0))
    return pl.pallas_call(
        functools.partial(_nsa_kernel, seq=s, n_top=min(SEL_TOP, n_sel)),
        out_shape=jax.ShapeDtypeStruct((b, g, nb, NSA_ROWS, HEAD_DIM), BF16),
        grid=(b, g, nb // NSA_QB),
        in_specs=[per_step(LANES), per_group(nc, LANES), per_group(nc, LANES),
                  per_group(s, 2 * LANES), per_group(s, LANES), per_group(s, LANES), per_group(s, LANES),
                  per_step(3), pl.BlockSpec((n_sel, nc), lambda b_, g_, i: (0, 0))],
        out_specs=per_step(HEAD_DIM),
        scratch_shapes=[pltpu.VMEM((n_sel, STEP_Q), F32), pltpu.VMEM((STEP_ROWS, 2 * LANES), BF16)]
        + _softmax_scratch(STEP_ROWS, KEY_TILE),
        compiler_params=_cparams(3), name="nsa_attention",
    )(q_aug, kc_aug, vc_aug, ks_aug, vs_aug, kw_aug, vw_aug, gate_logit, selmap)


def _causal_conv3(u, halo, w):
    row = lax.broadcasted_iota(jnp.int32, u.shape, 0)
    last = halo.shape[0] - 1
    prev1 = halo[last:last + 1]
    prev2 = halo[last - 1:last]
    u1 = jnp.where(row == 0, prev1, pltpu.roll(u, 1, 0))
    u2 = jnp.where(row == 0, prev2, jnp.where(row == 1, prev1, pltpu.roll(u, 2, 0)))
    return w[0:1] * u2 + w[1:2] * u1 + w[2:3] * u


MIX_ROWS = 512
MIX_COLS = 256


def _mixer_output_kernel(h_ref, wg_ref, bg_ref, cb_ref, cc_ref, cx_ref, cch_ref, cxh_ref, a_ref, c_ref,
                         wa_ref, wb_ref, wc_ref, cw_ref, wo_ref, x_ref, g_ref, xo_ref, ho_ref,
                         a_sc, b_sc, c_sc, mix_sc, *, tm, seq):
    d = x_ref.shape[1]
    for hh in range(FOX_HEADS):
        a_sc[:, hh * HEAD_DIM:(hh + 1) * HEAD_DIM] = a_ref[hh]
    for g in range(NSA_KV_GROUPS):
        for hg in range(NSA_HG):
            c0 = (g * NSA_HG + hg) * HEAD_DIM
            for qb in range(tm // Q_BLOCK):
                c_sc[qb * Q_BLOCK:(qb + 1) * Q_BLOCK, c0:c0 + HEAD_DIM] = c_ref[g, qb, hg * Q_BLOCK:(hg + 1) * Q_BLOCK, :]
    seq_start = (pl.program_id(0) * tm) % seq == 0
    u = cc_ref[...] * cx_ref[...]
    halo = jnp.where(seq_start, 0.0, cch_ref[...] * cxh_ref[...])
    b_sc[...] = (cb_ref[...] * _causal_conv3(u, halo, cw_ref[...])).astype(BF16)

    h = h_ref[...]
    for col in range(d // MIX_COLS):
        cols = slice(col * MIX_COLS, (col + 1) * MIX_COLS)
        mix = jnp.zeros((tm, MIX_COLS), F32)
        for branch, (src, w_ref) in enumerate(((a_sc, wa_ref), (b_sc, wb_ref), (c_sc, wc_ref))):
            gcols = slice(branch * d + col * MIX_COLS, branch * d + (col + 1) * MIX_COLS)
            gate = jnp.dot(h, wg_ref[:, gcols], preferred_element_type=F32) + bg_ref[:, gcols]
            mix = mix + _sigmoid(gate) * jnp.dot(src[...], w_ref[:, cols], preferred_element_type=F32)
        mix_sc[:, cols] = mix.astype(BF16)

    xn = x_ref[...] + jnp.dot(mix_sc[...], wo_ref[...], preferred_element_type=F32)
    xo_ref[...] = xn
    ho_ref[...] = _rms(xn, g_ref[...]).astype(ho_ref.dtype)


def mixer_output(h, w_gate, b_gate, zc, a_out, c_out, w_a, w_b, w_c, conv_w, w_o, x, g, seq):
    n, d = x.shape
    cw = w_b.shape[0]
    tm = min(MIX_ROWS, seq)
    tpb = seq // tm
    nb = tm // Q_BLOCK
    row = lambda width: pl.BlockSpec((tm, width), lambda i: (i, 0))
    ccol = lambda c: pl.BlockSpec((tm, cw), lambda i: (i, c))
    hcol = lambda c: pl.BlockSpec((SUBLANES, cw), lambda i: (jnp.maximum(i * (tm // SUBLANES) - 1, 0), c))
    once = lambda shape: pl.BlockSpec(shape, lambda i: (0, 0), pipeline_mode=pl.Buffered(1))
    return pl.pallas_call(
        functools.partial(_mixer_output_kernel, tm=tm, seq=seq),
        out_shape=(jax.ShapeDtypeStruct((n, d), F32), jax.ShapeDtypeStruct((n, d), BF16)),
        grid=(n // tm,),
        in_specs=[row(d), once(w_gate.shape), once((1, 3 * d)),
                  ccol(0), ccol(1), ccol(2), hcol(1), hcol(2),
                  pl.BlockSpec((None, FOX_HEADS, tm, HEAD_DIM), lambda i: (i // tpb, 0, i % tpb, 0)),
                  pl.BlockSpec((None, NSA_KV_GROUPS, nb, NSA_ROWS, HEAD_DIM), lambda i: (i // tpb, 0, i % tpb, 0, 0)),
                  once(w_a.shape), once(w_b.shape), once(w_c.shape), once(conv_w.shape), once(w_o.shape),
                  row(d), once((1, d))],
        out_specs=(row(d), row(d)),
        scratch_shapes=[pltpu.VMEM((tm, FOX_HEADS * HEAD_DIM), BF16), pltpu.VMEM((tm, cw), BF16),
                        pltpu.VMEM((tm, NSA_HEADS * HEAD_DIM), BF16), pltpu.VMEM((tm, d), BF16)],
        compiler_params=_cparams(1), name="mixer_output",
    )(h, w_gate, b_gate.reshape(1, -1), zc, zc, zc, zc, zc, a_out, c_out, w_a, w_b, w_c, conv_w, w_o,
      x, g.reshape(1, d))


def _ffn_up_kernel(h_ref, hh_ref, wu_ref, wv_ref, cw_ref, o_ref, *, tm, seq):
    seq_start = (pl.program_id(1) * tm) % seq == 0
    wu = wu_ref[...]
    u = jnp.dot(h_ref[...], wu, preferred_element_type=F32)
    v = jnp.dot(h_ref[...], wv_ref[...], preferred_element_type=F32)
    halo = jnp.where(seq_start, 0.0, jnp.dot(hh_ref[...], wu, preferred_element_type=F32))
    c = _causal_conv3(u, halo, cw_ref[...])
    o_ref[...] = (c * _sigmoid(c) * v).astype(o_ref.dtype)


def ffn_up(h, w_up, conv_w, seq, tm=1024):
    n, d = h.shape
    f = conv_w.shape[1]
    tm = min(tm, n)
    nj = 2 if (f // 2) % LANES == 0 else 1
    tn = f // nj
    return pl.pallas_call(
        functools.partial(_ffn_up_kernel, tm=tm, seq=seq),
        out_shape=jax.ShapeDtypeStruct((n, f), BF16),
        grid=(nj, n // tm),
        in_specs=[pl.BlockSpec((tm, d), lambda j, i: (i, 0)),
                  pl.BlockSpec((2 * SUBLANES, d), lambda j, i: (jnp.maximum(i * (tm // (2 * SUBLANES)) - 1, 0), 0)),
                  pl.BlockSpec((d, tn), lambda j, i: (0, j)),
                  pl.BlockSpec((d, tn), lambda j, i: (0, nj + j)),
                  pl.BlockSpec((conv_w.shape[0], tn), lambda j, i: (0, j))],
        out_specs=pl.BlockSpec((tm, tn), lambda j, i: (i, j)),
        compiler_params=_cparams(2), name="ffn_up",
    )(h, h, w_up, w_up, conv_w)


def _selection_map(n_sel, n_cmp_pad):
    sel_start = np.arange(n_sel)[:, None] * SEL_LEN
    cmp_start = np.arange(n_cmp_pad)[None, :] * CMP_STRIDE
    ov = np.minimum(sel_start + SEL_LEN, cmp_start + CMP_LEN) - np.maximum(sel_start, cmp_start)
    ov = np.clip(ov, 0, None) / CMP_STRIDE
    ov[:, n_cmp_pad - 1] = 0.0
    return ov.astype(np.float32)


def _bf16_exact(a):
    a = np.asarray(a, np.float32)
    assert np.array_equal(a, a.astype(BF16).astype(np.float32)), "constant is not exact in bf16"
    return a


def _after_head(cols, width=LANES):
    out = np.zeros(cols.shape[:-1] + (width,), np.float32)
    out[..., HEAD_DIM:HEAD_DIM + cols.shape[-1]] = cols
    return out


def _alibi_query_cols(nqb):
    slope = np.array([2.0 ** (-8.0 * (i + 1) / NSA_HEADS) for i in range(NSA_HEADS)], np.float32)
    slope = slope.reshape(NSA_KV_GROUPS, 1, NSA_HG, 1)
    qpos = (np.arange(nqb)[:, None] * Q_BLOCK + np.arange(Q_BLOCK)[None, :]).reshape(1, nqb, 1, Q_BLOCK)
    cols = np.stack(np.broadcast_arrays(slope * SEL_LEN, slope, -slope * SEL_LEN * (qpos // SEL_LEN),
                                        -slope * (qpos % SEL_LEN)), axis=-1)
    return _bf16_exact(cols.reshape(NSA_KV_GROUPS, nqb, NSA_ROWS, 4))


def _alibi_key_cols(pos):
    pos = np.asarray(pos)
    return _bf16_exact(np.stack([pos // SEL_LEN, pos % SEL_LEN, np.ones_like(pos), np.ones_like(pos)], axis=-1))


def _split_w_in(w_in_l, b_in_l, d):
    sizes = (d, d, d, 512, 512, 512, FOX_HEADS, 512, 512, 512, 512, 128, 128, 128, 128, 128, 128, NSA_HEADS * 3)
    offs = np.concatenate([[0], np.cumsum(sizes)])
    col = lambda a, k: a[..., int(offs[k]):int(offs[k + 1])]
    (GA, GB, GC, FQ, FK, FV, FF, CB, CC, CX, NQ, NKC, NVC, NKS, NVS, NKW, NVW, NG) = range(18)
    pad = LANES - (FOX_HEADS + NSA_HEADS * 3)

    prescale = {FQ: SCALE, NQ: SCALE}

    def group(idx, extra=0):
        w = jnp.concatenate([col(w_in_l, k) * prescale.get(k, 1.0) for k in idx], axis=-1)
        bias = jnp.concatenate([col(b_in_l, k) * prescale.get(k, 1.0) for k in idx], axis=-1)
        if extra:
            w = jnp.pad(w, ((0, 0), (0, extra)))
            bias = jnp.pad(bias, ((0, extra),))
        return w.astype(BF16), bias

    return (group((FQ, FK, FV)), group((NQ, NKS, NVS, NKW, NVW, NKC, NVC)),
            group((GA, GB, GC)), group((CB, CC, CX)), group((FF, NG), pad))


def kernel(x, norm_mix_g, w_in, b_in, cmp_pe_k, cmp_w1_k, cmp_w2_k, cmp_pe_v, cmp_w1_v, cmp_w2_v, sc_conv_w, w_br_a, w_br_b, w_br_c, w_o, norm_ffn_g, w_up, ffn_conv_w, w_down, norm_final_g):
    b, s, d = x.shape
    depth = w_in.shape[0]
    n = b * s
    nqb = s // Q_BLOCK
    n_cmp_pad = s // CMP_STRIDE
    const = lambda a: jnp.asarray(a, dtype=BF16)
    selmap = const(_selection_map(s // SEL_LEN, n_cmp_pad))
    one_col = const(_after_head(np.ones((1, 1), np.float32)))
    q_cols = const(_after_head(_alibi_query_cols(nqb)))
    cmp_cols = const(_after_head(_alibi_key_cols(np.arange(n_cmp_pad) * CMP_STRIDE + CMP_LEN - 1)))
    key_cols = _after_head(_alibi_key_cols(np.arange(s)))
    key_block = (np.arange(s)[:, None] // SEL_LEN == np.arange(LANES)[None, :]).astype(np.float32)
    sel_cols = const(np.concatenate([key_cols, key_block], axis=-1))
    key_cols = const(key_cols)

    xr = x.reshape(n, d)
    h = rmsnorm_rows(xr, norm_mix_g[0], BF16)
    for l in range(depth):
        ((w_fox, b_fox), (w_nsa, b_nsa), (w_gate, b_gate), (w_conv, b_conv),
         (w_sm, b_sm)) = _split_w_in(w_in[l], b_in[l], d)
        zc = matmul_bias(h, w_conv, b_conv, F32, tn=w_conv.shape[1])
        zs3 = matmul_bias(h, w_sm, b_sm, F32, tn=LANES).reshape(b, s, -1)

        bias_cols = fox_bias_columns(zs3[..., 0:FOX_HEADS].transpose(0, 2, 1))
        a_out = fox_attention(fox_qkv_projection(h, w_fox, b_fox, bias_cols, b, s))

        q_aug, ks_aug, vs_aug, kw_aug, vw_aug, cmp_in = nsa_projection(
            h, w_nsa, b_nsa, q_cols, sel_cols, key_cols, one_col, b, s)
        kc_aug, vc_aug = compress_blocks(
            cmp_in.reshape(2, b, NSA_KV_GROUPS, n_cmp_pad, CMP_STRIDE * HEAD_DIM),
            cmp_pe_k[l].reshape(1, -1), cmp_pe_v[l].reshape(1, -1),
            cmp_w1_k[l].astype(BF16), cmp_w2_k[l].astype(BF16),
            cmp_w1_v[l].astype(BF16), cmp_w2_v[l].astype(BF16), cmp_cols, one_col)
        ng = zs3[..., FOX_HEADS:FOX_HEADS + NSA_HEADS * 3].reshape(b, nqb, Q_BLOCK, NSA_KV_GROUPS, NSA_HG, 3)
        ng = ng.transpose(0, 3, 1, 4, 2, 5).reshape(b, NSA_KV_GROUPS, nqb, NSA_ROWS, 3)
        c_out = nsa_attention(q_aug, kc_aug, vc_aug, ks_aug, vs_aug, kw_aug, vw_aug, ng, selmap)

        xr, h2 = mixer_output(h, w_gate, b_gate, zc, a_out, c_out, w_br_a[l].astype(BF16),
                              w_br_b[l].astype(BF16), w_br_c[l].astype(BF16), sc_conv_w[l],
                              w_o[l].astype(BF16), xr, norm_ffn_g[l], s)

        act = ffn_up(h2, w_up[l].astype(BF16), ffn_conv_w[l], s)
        if l + 1 < depth:
            xr, h = residual_matmul_norm(act, w_down[l].astype(BF16), xr, norm_mix_g[l + 1], BF16)
        else:
            out = residual_matmul_norm(act, w_down[l].astype(BF16), xr, norm_final_g, F32,
                                       keep_residual=False)
    return out.reshape(b, s, d)
```

```python
import functools

import numpy as np
import jax
import jax.numpy as jnp
from jax import lax
from jax.experimental import pallas as pl
from jax.experimental.pallas import tpu as pltpu

F32 = jnp.float32
BF16 = jnp.bfloat16

HEAD_DIM = 64
FOX_HEADS = 8
NSA_HEADS = 8
NSA_KV_GROUPS = 2
NSA_HG = NSA_HEADS // NSA_KV_GROUPS
CMP_LEN = 32
CMP_STRIDE = 16
SEL_LEN = 64
SEL_TOP = 16
WIN = 512
Q_BLOCK = 128
EPS = 1e-6
NEG_INF = -1e30
FORCE_BONUS = 1e4
SCALE = HEAD_DIM ** -0.5

LANES = 128
SUBLANES = 8
VMEM_LIMIT = 48 * 1024 * 1024
PROJ_ROWS = 1024

NT_DIMS = (((1,), (1,)), ((), ()))


def _cparams(n_parallel, n_arbitrary=0):
    return pltpu.CompilerParams(
        dimension_semantics=("parallel",) * n_parallel + ("arbitrary",) * n_arbitrary,
        vmem_limit_bytes=VMEM_LIMIT)


def _sigmoid(x):
    return 1.0 / (1.0 + jnp.exp(-x))


def _rms(x, g):
    ms = jnp.mean(x * x, axis=-1, keepdims=True)
    return x * lax.rsqrt(ms + EPS) * g


def _widen(x, width):
    return jnp.concatenate([x, jnp.zeros((x.shape[0], width - x.shape[1]), x.dtype)], axis=1).astype(BF16)


def _norm_kernel(x_ref, g_ref, o_ref):
    o_ref[...] = _rms(x_ref[...], g_ref[...]).astype(o_ref.dtype)


def rmsnorm_rows(x, g, out_dtype, tm=1024):
    n, d = x.shape
    tm = min(tm, n)
    return pl.pallas_call(
        _norm_kernel,
        out_shape=jax.ShapeDtypeStruct((n, d), out_dtype),
        grid=(n // tm,),
        in_specs=[pl.BlockSpec((tm, d), lambda i: (i, 0)),
                  pl.BlockSpec((1, d), lambda i: (0, 0))],
        out_specs=pl.BlockSpec((tm, d), lambda i: (i, 0)),
        compiler_params=_cparams(1),
        name="rmsnorm",
    )(x, g.reshape(1, d))


def _mm_bias_kernel(a_ref, w_ref, b_ref, o_ref):
    acc = jnp.dot(a_ref[...], w_ref[...], preferred_element_type=F32)
    o_ref[...] = (acc + b_ref[...]).astype(o_ref.dtype)


def matmul_bias(a, w, b, out_dtype, tn, tm=PROJ_ROWS):
    n, k = a.shape
    m = w.shape[1]
    tm = min(tm, n)
    return pl.pallas_call(
        _mm_bias_kernel,
        out_shape=jax.ShapeDtypeStruct((n, m), out_dtype),
        grid=(n // tm, m // tn),
        in_specs=[pl.BlockSpec((tm, k), lambda i, j: (i, 0)),
                  pl.BlockSpec((k, tn), lambda i, j: (0, j)),
                  pl.BlockSpec((1, tn), lambda i, j: (0, j))],
        out_specs=pl.BlockSpec((tm, tn), lambda i, j: (i, j)),
        compiler_params=_cparams(2),
        name="matmul_bias",
    )(a, w, b.reshape(1, m))


HEAD_PROJ_ROWS = 512


def _fox_qkv_kernel(a_ref, w_ref, b_ref, aug_ref, o_ref):
    acc = jnp.dot(a_ref[...], w_ref[...], preferred_element_type=F32) + b_ref[...]
    lane = lax.broadcasted_iota(jnp.int32, (acc.shape[0], LANES), 1)
    ones_col = jnp.where(lane == HEAD_DIM, 1.0, 0.0).astype(BF16)
    for plane in range(3):
        for hh in range(FOX_HEADS):
            c0 = (plane * FOX_HEADS + hh) * HEAD_DIM
            extra = ones_col if plane == 2 else aug_ref[plane, hh]
            o_ref[plane, hh] = _widen(acc[:, c0:c0 + HEAD_DIM], LANES) + extra


def fox_qkv_projection(h, w, b, aug, batch, seq):
    n, d = h.shape
    tm = min(HEAD_PROJ_ROWS, seq)
    tpb = seq // tm
    plane_block = lambda planes: pl.BlockSpec((planes, None, FOX_HEADS, tm, LANES),
                                              lambda i: (0, i // tpb, 0, i % tpb, 0))
    return pl.pallas_call(
        _fox_qkv_kernel,
        out_shape=jax.ShapeDtypeStruct((3, batch, FOX_HEADS, seq, LANES), BF16),
        grid=(n // tm,),
        in_specs=[pl.BlockSpec((tm, d), lambda i: (i, 0)),
                  pl.BlockSpec(w.shape, lambda i: (0, 0)),
                  pl.BlockSpec((1, w.shape[1]), lambda i: (0, 0)),
                  plane_block(2)],
        out_specs=plane_block(3),
        compiler_params=_cparams(1), name="fox_qkv_projection",
    )(h, w, b.reshape(1, -1), aug)


def _nsa_proj_kernel(a_ref, w_ref, b_ref, qcols_ref, selcols_ref, keycols_ref, onecol_ref,
                     q_ref, ks_ref, vs_ref, kw_ref, vw_ref, cmp_ref):
    acc = jnp.dot(a_ref[...], w_ref[...], preferred_element_type=F32) + b_ref[...]
    for g in range(NSA_KV_GROUPS):
        for hg in range(NSA_HG):
            c0 = (g * NSA_HG + hg) * HEAD_DIM
            for qb in range(acc.shape[0] // Q_BLOCK):
                rows = slice(hg * Q_BLOCK, (hg + 1) * Q_BLOCK)
                q_ref[g, qb, rows, :] = (_widen(acc[qb * Q_BLOCK:(qb + 1) * Q_BLOCK, c0:c0 + HEAD_DIM], LANES)
                                         + qcols_ref[g, qb, rows, :])
    gw = NSA_KV_GROUPS * HEAD_DIM
    kv0 = NSA_HEADS * HEAD_DIM
    for g in range(NSA_KV_GROUPS):
        part = lambda idx: acc[:, kv0 + idx * gw + g * HEAD_DIM: kv0 + idx * gw + (g + 1) * HEAD_DIM]
        ks_ref[g] = _widen(part(0), 2 * LANES) + selcols_ref[...]
        vs_ref[g] = _widen(part(1), LANES) + onecol_ref[...]
        kw_ref[g] = _widen(part(2), LANES) + keycols_ref[...]
        vw_ref[g] = _widen(part(3), LANES) + onecol_ref[...]
        cmp_ref[0, g] = part(4)
        cmp_ref[1, g] = part(5)


def nsa_projection(h, w, b, q_cols, sel_cols, key_cols, one_col, batch, seq):
    n, d = h.shape
    tm = min(HEAD_PROJ_ROWS, seq)
    tpb = seq // tm
    nb = tm // Q_BLOCK
    rows = NSA_HG * Q_BLOCK
    out = lambda width: jax.ShapeDtypeStruct((batch, NSA_KV_GROUPS, seq, width), BF16)
    out_spec = lambda width: pl.BlockSpec((None, NSA_KV_GROUPS, tm, width), lambda i: (i // tpb, 0, i % tpb, 0))
    return pl.pallas_call(
        _nsa_proj_kernel,
        out_shape=(jax.ShapeDtypeStruct((batch, NSA_KV_GROUPS, seq // Q_BLOCK, rows, LANES), BF16),
                   out(2 * LANES), out(LANES), out(LANES), out(LANES),
                   jax.ShapeDtypeStruct((2, batch, NSA_KV_GROUPS, seq, HEAD_DIM), F32)),
        grid=(n // tm,),
        in_specs=[pl.BlockSpec((tm, d), lambda i: (i, 0)),
                  pl.BlockSpec(w.shape, lambda i: (0, 0)),
                  pl.BlockSpec((1, w.shape[1]), lambda i: (0, 0)),
                  pl.BlockSpec((NSA_KV_GROUPS, nb, rows, LANES), lambda i: (0, i % tpb, 0, 0)),
                  pl.BlockSpec((tm, 2 * LANES), lambda i: (i % tpb, 0)),
                  pl.BlockSpec((tm, LANES), lambda i: (i % tpb, 0)),
                  pl.BlockSpec((1, LANES), lambda i: (0, 0))],
        out_specs=(pl.BlockSpec((None, NSA_KV_GROUPS, nb, rows, LANES), lambda i: (i // tpb, 0, i % tpb, 0, 0)),
                   out_spec(2 * LANES), out_spec(LANES), out_spec(LANES), out_spec(LANES),
                   pl.BlockSpec((2, None, NSA_KV_GROUPS, tm, HEAD_DIM), lambda i: (0, i // tpb, 0, i % tpb, 0))),
        compiler_params=_cparams(1), name="nsa_projection",
    )(h, w, b.reshape(1, -1), q_cols, sel_cols, key_cols, one_col)


def _res_norm_kernel(a_ref, w_ref, x_ref, g_ref, xo_ref, ho_ref):
    xn = x_ref[...] + jnp.dot(a_ref[...], w_ref[...], preferred_element_type=F32)
    xo_ref[...] = xn
    ho_ref[...] = _rms(xn, g_ref[...]).astype(ho_ref.dtype)


def _res_norm_only_kernel(a_ref, w_ref, x_ref, g_ref, ho_ref):
    xn = x_ref[...] + jnp.dot(a_ref[...], w_ref[...], preferred_element_type=F32)
    ho_ref[...] = _rms(xn, g_ref[...]).astype(ho_ref.dtype)


def residual_matmul_norm(a, w, x, g, norm_dtype, keep_residual=True, tm=512):
    n, k = a.shape
    d = w.shape[1]
    tm = min(tm, n)
    row = lambda i: (i, 0)
    in_specs = [pl.BlockSpec((tm, k), row),
                pl.BlockSpec((k, d), lambda i: (0, 0)),
                pl.BlockSpec((tm, d), row),
                pl.BlockSpec((1, d), lambda i: (0, 0))]
    if keep_residual:
        return pl.pallas_call(
            _res_norm_kernel,
            out_shape=(jax.ShapeDtypeStruct((n, d), F32), jax.ShapeDtypeStruct((n, d), norm_dtype)),
            grid=(n // tm,), in_specs=in_specs,
            out_specs=(pl.BlockSpec((tm, d), row), pl.BlockSpec((tm, d), row)),
            compiler_params=_cparams(1), name="residual_matmul_norm",
        )(a, w, x, g.reshape(1, d))
    return pl.pallas_call(
        _res_norm_only_kernel,
        out_shape=jax.ShapeDtypeStruct((n, d), norm_dtype),
        grid=(n // tm,), in_specs=in_specs,
        out_specs=pl.BlockSpec((tm, d), row),
        compiler_params=_cparams(1), name="residual_matmul_final_norm",
    )(a, w, x, g.reshape(1, d))


CUM_BLOCK = 256


def _bf16_pieces(x):
    hi = x.astype(BF16).astype(F32)
    r1 = x - hi
    mid = r1.astype(BF16).astype(F32)
    lo = (r1 - mid).astype(BF16).astype(F32)
    return hi, mid, lo


CUM_CHUNK = 1024


def _fox_bias_columns_kernel(f_ref, o_ref, carry_sc):
    heads, chunk = f_ref.shape

    @pl.when(pl.program_id(1) == 0)
    def _():
        carry_sc[...] = jnp.zeros_like(carry_sc)

    x = f_ref[...]
    ls = jnp.minimum(x, 0.0) - jnp.log1p(jnp.exp(-jnp.abs(x)))
    r = lax.broadcasted_iota(jnp.int32, (CUM_BLOCK, CUM_BLOCK), 0)
    c = lax.broadcasted_iota(jnp.int32, (CUM_BLOCK, CUM_BLOCK), 1)
    tri = jnp.where(r <= c, 1.0, 0.0).astype(BF16)
    eye = jnp.where(r == c, 1.0, 0.0).astype(BF16)
    lane_q = lax.broadcasted_iota(jnp.int32, (2 * LANES, CUM_BLOCK), 0) - HEAD_DIM
    lane_k = lane_q - LANES
    ones_rows = jnp.where(((lane_q >= 3) & (lane_q < 6)) | ((lane_k >= 0) & (lane_k < 3)), 1.0, 0.0)
    carry = carry_sc[...]
    for blk in range(chunk // CUM_BLOCK):
        cols = slice(blk * CUM_BLOCK, (blk + 1) * CUM_BLOCK)
        cs = carry
        for piece in _bf16_pieces(ls[:, cols]):
            cs = cs + jnp.dot(piece.astype(BF16), tri, preferred_element_type=F32)
        carry = cs[:, CUM_BLOCK - 1:CUM_BLOCK]
        pieces = _bf16_pieces(cs)
        for hh in range(heads):
            both = ones_rows
            for k, piece in enumerate(pieces):
                both = jnp.where(lane_q == k, piece[hh:hh + 1], both)
                both = jnp.where(lane_k == 3 + k, -piece[hh:hh + 1], both)
            cols_t = lax.dot_general(eye, both.astype(BF16), NT_DIMS, preferred_element_type=F32)
            o_ref[0, hh, cols, :] = cols_t[:, :LANES].astype(o_ref.dtype)
            o_ref[1, hh, cols, :] = cols_t[:, LANES:].astype(o_ref.dtype)
    carry_sc[...] = carry


def fox_bias_columns(f_t):
    b, h, s = f_t.shape
    chunk = min(CUM_CHUNK, s)
    return pl.pallas_call(
        _fox_bias_columns_kernel,
        out_shape=jax.ShapeDtypeStruct((2, b, h, s, LANES), BF16),
        grid=(b, s // chunk),
        in_specs=[pl.BlockSpec((None, h, chunk), lambda i, j: (i, 0, j))],
        out_specs=pl.BlockSpec((2, None, h, chunk, LANES), lambda i, j: (0, i, 0, j, 0)),
        scratch_shapes=[pltpu.VMEM((h, 1), F32)],
        compiler_params=_cparams(1, 1), name="fox_bias_columns",
    )(f_t)


KEY_TILE = 512
SOFTMAX_ROWS = 64


def _online_softmax_tiles(n_full, logits, values, last_bias, s_sc, p_sc, m_sc, a_sc, acc_sc):
    rows_total, tile = p_sc.shape
    rep = tile // LANES
    m_sc[...] = jnp.full_like(m_sc, NEG_INF)
    acc_sc[...] = jnp.zeros_like(acc_sc)

    def update(slot, t, last):
        for c in range(rows_total // SOFTMAX_ROWS):
            rows = pl.ds(c * SOFTMAX_ROWS, SOFTMAX_ROWS)
            s = s_sc[slot, rows, :]
            if last:
                s = last_bias(s, c * SOFTMAX_ROWS)
            m_prev = m_sc[rows, :]
            m_new = jnp.maximum(m_prev, jnp.max(s, axis=-1, keepdims=True))
            a_sc[rows, :] = jnp.exp(m_prev - m_new)
            m_sc[rows, :] = m_new
            p_sc[rows, :] = jnp.exp(s - jnp.concatenate([m_new] * rep, axis=1)).astype(BF16)
        acc_sc[...] = a_sc[...] * acc_sc[...] + jnp.dot(p_sc[...], values(t), preferred_element_type=F32)

    s_sc[0] = logits(0)

    def two_tiles(u, carry):
        t = 2 * u
        s_sc[1] = logits(t + 1)
        update(0, t, False)
        s_sc[0] = logits(t + 2)
        update(1, t + 1, False)
        return carry

    lax.fori_loop(0, n_full // 2, two_tiles, 0)

    @pl.when(n_full % 2 == 1)
    def _():
        s_sc[1] = logits(n_full)
        update(0, n_full - 1, False)
        update(1, n_full, True)

    @pl.when(n_full % 2 == 0)
    def _():
        update(0, n_full, True)


def _softmax_scratch(rows, tile):
    return [pltpu.VMEM((2, rows, tile), F32), pltpu.VMEM((rows, tile), BF16),
            pltpu.VMEM((rows, LANES), F32), pltpu.VMEM((rows, LANES), F32), pltpu.VMEM((rows, LANES), F32)]


def _normalized(acc):
    return acc[:, :HEAD_DIM] / acc[:, HEAD_DIM:HEAD_DIM + 1]


def _fox_kernel(q_ref, k_ref, v_ref, o_ref, s_sc, p_sc, m_sc, a_sc, acc_sc):
    tile = KEY_TILE
    i = pl.program_id(2)
    q = q_ref[...]

    def logits(t):
        k0 = pl.multiple_of(t * tile, tile)
        return lax.dot_general(q, k_ref[pl.ds(k0, tile), :], NT_DIMS, preferred_element_type=F32)

    def values(t):
        return v_ref[pl.ds(pl.multiple_of(t * tile, tile), tile), :]

    def causal(s, first_row):
        qi = first_row + lax.broadcasted_iota(jnp.int32, s.shape, 0)
        ki = lax.broadcasted_iota(jnp.int32, s.shape, 1)
        return jnp.where(ki <= qi, s, NEG_INF)

    _online_softmax_tiles(i, logits, values, causal, s_sc, p_sc, m_sc, a_sc, acc_sc)
    o_ref[...] = _normalized(acc_sc[...]).astype(o_ref.dtype)


def fox_attention(qkv):
    _, b, h, s, _ = qkv.shape
    tile = KEY_TILE
    assert s % tile == 0
    whole = lambda plane: pl.BlockSpec((None, None, None, s, LANES), lambda b_, h_, i: (plane, b_, h_, 0, 0))
    return pl.pallas_call(
        _fox_kernel,
        out_shape=jax.ShapeDtypeStruct((b, h, s, HEAD_DIM), BF16),
        grid=(b, h, s // tile),
        in_specs=[pl.BlockSpec((None, None, None, tile, LANES), lambda b_, h_, i: (0, b_, h_, i, 0)),
                  whole(1), whole(2)],
        out_specs=pl.BlockSpec((None, None, tile, HEAD_DIM), lambda b_, h_, i: (b_, h_, i, 0)),
        scratch_shapes=_softmax_scratch(tile, tile),
        compiler_params=_cparams(3), name="fox_attention",
    )(qkv, qkv, qkv)


def _gelu_tanh(x):
    return 0.5 * x * (1.0 + jnp.tanh(np.sqrt(2.0 / np.pi).astype(np.float32) * (x + 0.044715 * (x * x * x))))


def _compress_kernel(xk_ref, xv_ref, pek_ref, pev_ref, w1k_ref, w2k_ref, w1v_ref, w2v_ref,
                     kcols_ref, onecol_ref, ko_ref, vo_ref):
    nc, half = xk_ref.shape

    def mlp(x_ref, pe_ref, w1_ref, w2_ref):
        x = x_ref[...]
        first = jnp.dot((x + pe_ref[:, :half]).astype(BF16), w1_ref[:half, :], preferred_element_type=F32)
        second = jnp.dot((x + pe_ref[:, half:]).astype(BF16), w1_ref[half:, :], preferred_element_type=F32)
        hid = _gelu_tanh(first + pltpu.roll(second, nc - 1, 0))
        return jnp.dot(hid.astype(BF16), w2_ref[...], preferred_element_type=F32)

    ko_ref[...] = _widen(mlp(xk_ref, pek_ref, w1k_ref, w2k_ref), LANES) + kcols_ref[...]
    vo_ref[...] = _widen(mlp(xv_ref, pev_ref, w1v_ref, w2v_ref), LANES) + onecol_ref[...]


def compress_blocks(chunks, pe_k, pe_v, w1_k, w2_k, w1_v, w2_v, key_cols, one_col):
    _, b, g, nc, half = chunks.shape
    feat = 2 * half
    chunk_spec = lambda plane: pl.BlockSpec((None, None, None, nc, half), lambda i, j: (plane, i, j, 0, 0))
    full = lambda shape: pl.BlockSpec(shape, lambda i, j: (0,) * len(shape))
    out = jax.ShapeDtypeStruct((b, g, nc, LANES), BF16)
    out_spec = pl.BlockSpec((None, None, nc, LANES), lambda i, j: (i, j, 0, 0))
    return pl.pallas_call(
        _compress_kernel,
        out_shape=(out, out),
        grid=(b, g),
        in_specs=[chunk_spec(0), chunk_spec(1), full((1, feat)), full((1, feat)),
                  full((feat, HEAD_DIM)), full((HEAD_DIM, HEAD_DIM)),
                  full((feat, HEAD_DIM)), full((HEAD_DIM, HEAD_DIM)),
                  full((nc, LANES)), full((1, LANES))],
        out_specs=(out_spec, out_spec),
        compiler_params=_cparams(2), name="nsa_compress",
    )(chunks, chunks, pe_k, pe_v, w1_k, w2_k, w1_v, w2_v, key_cols, one_col)


NSA_ROWS = NSA_HG * Q_BLOCK
NSA_QB = 4
STEP_Q = NSA_QB * Q_BLOCK
STEP_ROWS = NSA_QB * NSA_ROWS


def _select_blocks_t(score_ref, n_sel, n_top):
    nq = score_ref.shape[1]
    groups = n_sel // SUBLANES
    grp = [score_ref[pl.ds(g * SUBLANES, SUBLANES), :] for g in range(groups)]
    blk = lax.broadcasted_iota(jnp.int32, (SUBLANES, nq), 0)
    beaten_by = [jnp.zeros((SUBLANES, nq), F32) for _ in range(groups)]
    for k in range(n_sel):
        sk = jnp.broadcast_to(score_ref[pl.ds(k, 1), :], (SUBLANES, nq))
        for g in range(groups):
            if g * SUBLANES > k:
                wins = sk >= grp[g]
            elif g * SUBLANES + SUBLANES - 1 < k:
                wins = sk > grp[g]
            else:
                wins = (sk > grp[g]) | ((sk == grp[g]) & (blk + g * SUBLANES > k))
            beaten_by[g] = beaten_by[g] + jnp.where(wins, 1.0, 0.0)
    return jnp.concatenate([jnp.where(c < n_top, 1.0, 0.0) for c in beaten_by], axis=0)


def _nsa_kernel(q_ref, kc_ref, vc_ref, ks_ref, vs_ref, kw_ref, vw_ref, gl_ref, selmap_ref, o_ref,
                score_sc, qa_sc, s_sc, p_sc, m_sc, a_sc, acc_sc, *, seq, n_top):
    i = pl.program_id(2)
    q0 = i * STEP_Q
    n_cmp_pad = seq // CMP_STRIDE
    n_sel = seq // SEL_LEN
    q1 = q_ref[...].reshape(STEP_ROWS, LANES)
    row = lax.broadcasted_iota(jnp.int32, (STEP_ROWS, 1), 0)

    def query_pos(r):
        return q0 + (r // NSA_ROWS) * Q_BLOCK + (r & (Q_BLOCK - 1))

    qpos_col = query_pos(row)

    cend = lax.broadcasted_iota(jnp.int32, (1, n_cmp_pad), 1) * CMP_STRIDE + (CMP_LEN - 1)
    valid_c = cend <= qpos_col
    lc = jnp.where(valid_c, lax.dot_general(q1, kc_ref[...], NT_DIMS, preferred_element_type=F32), NEG_INF)
    mc = jnp.max(lc, axis=-1, keepdims=True)
    pc = jnp.where(valid_c, jnp.exp(lc - mc), 0.0)
    sc = jnp.sum(pc, axis=-1, keepdims=True)
    pc = pc / jnp.where(sc > 0, sc, 1.0)
    o_c = jnp.dot(pc.astype(BF16), vc_ref[...], preferred_element_type=F32)[:, :HEAD_DIM]

    per_block = []
    for qb in range(NSA_QB):
        base = qb * NSA_ROWS
        acc_q = pc[base:base + Q_BLOCK]
        for hh in range(1, NSA_HG):
            acc_q = acc_q + pc[base + hh * Q_BLOCK:base + (hh + 1) * Q_BLOCK]
        per_block.append(acc_q)
    pcs = jnp.concatenate(per_block, axis=0)
    hi = pcs.astype(BF16)
    lo = (pcs - hi.astype(F32)).astype(BF16)
    selmap = selmap_ref[...]
    imp_t = (lax.dot_general(selmap, hi, NT_DIMS, preferred_element_type=F32)
             + lax.dot_general(selmap, lo, NT_DIMS, preferred_element_type=F32))
    blk = lax.broadcasted_iota(jnp.int32, (n_sel, STEP_Q), 0)
    qrow = q0 + lax.broadcasted_iota(jnp.int32, (n_sel, STEP_Q), 1)
    qblk = qrow >> 6
    forced = (blk == 0) | (blk == qblk) | (blk == qblk - 1)
    score_sc[...] = jnp.where(blk * SEL_LEN <= qrow, imp_t + jnp.where(forced, FORCE_BONUS, 0.0), NEG_INF)
    sel_t = _select_blocks_t(score_sc, n_sel, n_top)

    sel_t = jnp.concatenate([sel_t, jnp.ones((LANES - n_sel, STEP_Q), F32)], axis=0).astype(BF16)
    eye = jnp.where(lax.broadcasted_iota(jnp.int32, (STEP_Q, STEP_Q), 0)
                    == lax.broadcasted_iota(jnp.int32, (STEP_Q, STEP_Q), 1), 1.0, 0.0).astype(BF16)
    sel = lax.dot_general(eye, sel_t, NT_DIMS, preferred_element_type=F32)
    closed = jnp.where(sel > 0.5, 0.0, NEG_INF).astype(BF16)
    qa_sc[:, 0:LANES] = q1
    for qb in range(NSA_QB):
        for hh in range(NSA_HG):
            r0 = qb * NSA_ROWS + hh * Q_BLOCK
            qa_sc[r0:r0 + Q_BLOCK, LANES:2 * LANES] = closed[qb * Q_BLOCK:(qb + 1) * Q_BLOCK]

    band = WIN + STEP_Q
    w0 = pl.multiple_of(jnp.maximum(q0 - WIN, 0), Q_BLOCK)
    rel_w = (w0 + lax.broadcasted_iota(jnp.int32, (STEP_Q, band), 1)
             - (q0 + lax.broadcasted_iota(jnp.int32, (STEP_Q, band), 0)))
    bias_w = jnp.where((rel_w <= 0) & (rel_w > -WIN), 0.0, NEG_INF).reshape(NSA_QB, 1, Q_BLOCK, band)
    sw = lax.dot_general(q1, kw_ref[pl.ds(w0, band), :], NT_DIMS, preferred_element_type=F32)
    sw = (sw.reshape(NSA_QB, NSA_HG, Q_BLOCK, band) + bias_w).reshape(STEP_ROWS, band)
    pw = jnp.exp(sw - jnp.max(sw, axis=-1, keepdims=True))
    o_w = _normalized(jnp.dot(pw.astype(BF16), vw_ref[pl.ds(w0, band), :], preferred_element_type=F32))
    gate = _sigmoid(gl_ref[...].reshape(STEP_ROWS, 3))
    out_cw = gate[:, 0:1] * o_c + gate[:, 2:3] * o_w

    tile = KEY_TILE
    n_full = q0 // tile

    def logits(t):
        k0 = pl.multiple_of(t * tile, tile)
        return lax.dot_general(qa_sc[...], ks_ref[pl.ds(k0, tile), :], NT_DIMS, preferred_element_type=F32)

    def values(t):
        return vs_ref[pl.ds(pl.multiple_of(t * tile, tile), tile), :]

    def causal(s, first_row):
        r = first_row + lax.broadcasted_iota(jnp.int32, s.shape, 0)
        kpos = n_full * tile + lax.broadcasted_iota(jnp.int32, s.shape, 1)
        return jnp.where(kpos <= query_pos(r), s, NEG_INF)

    _online_softmax_tiles(n_full, logits, values, causal, s_sc, p_sc, m_sc, a_sc, acc_sc)
    o_s = _normalized(acc_sc[...])
    o_ref[...] = (out_cw + gate[:, 1:2] * o_s).astype(o_ref.dtype).reshape(NSA_QB, NSA_ROWS, HEAD_DIM)


def nsa_attention(q_aug, kc_aug, vc_aug, ks_aug, vs_aug, kw_aug, vw_aug, gate_logit, selmap):
    b, g, nb = q_aug.shape[:3]
    s = vs_aug.shape[2]
    nc = kc_aug.shape[2]
    n_sel = s // SEL_LEN
    assert s % KEY_TILE == 0 and KEY_TILE % STEP_Q == 0 and s >= WIN + STEP_Q and n_sel <= LANES
    per_step = lambda width: pl.BlockSpec((None, None, NSA_QB, NSA_ROWS, width),
                                          lambda b_, g_, i: (b_, g_, i, 0, 0))
    per_group = lambda rows, width: pl.BlockSpec((None, None, rows, width), lambda b_, g_, i: (b_, g_, 0, 0))
    return pl.pallas_call(
        functools.partial(_nsa_kernel, seq=s, n_top=min(SEL_TOP, n_sel)),
        out_shape=jax.ShapeDtypeStruct((b, g, nb, NSA_ROWS, HEAD_DIM), BF16),
        grid=(b, g, nb // NSA_QB),
        in_specs=[per_step(LANES), per_group(nc, LANES), per_group(nc, LANES),
                  per_group(s, 2 * LANES), per_group(s, LANES), per_group(s, LANES), per_group(s, LANES),
                  per_step(3), pl.BlockSpec((n_sel, nc), lambda b_, g_, i: (0, 0))],
        out_specs=per_step(HEAD_DIM),
        scratch_shapes=[pltpu.VMEM((n_sel, STEP_Q), F32), pltpu.VMEM((STEP_ROWS, 2 * LANES), BF16)]
        + _softmax_scratch(STEP_ROWS, KEY_TILE),
        compiler_params=_cparams(3), name="nsa_attention",
    )(q_aug, kc_aug, vc_aug, ks_aug, vs_aug, kw_aug, vw_aug, gate_logit, selmap)


def _causal_conv3(u, halo, w):
    row = lax.broadcasted_iota(jnp.int32, u.shape, 0)
    last = halo.shape[0] - 1
    prev1 = halo[last:last + 1]
    prev2 = halo[last - 1:last]
    u1 = jnp.where(row == 0, prev1, pltpu.roll(u, 1, 0))
    u2 = jnp.where(row == 0, prev2, jnp.where(row == 1, prev1, pltpu.roll(u, 2, 0)))
    return w[0:1] * u2 + w[1:2] * u1 + w[2:3] * u


MIX_ROWS = 512
MIX_COLS = 256


def _mixer_output_kernel(h_ref, wg_ref, bg_ref, cb_ref, cc_ref, cx_ref, cch_ref, cxh_ref, a_ref, c_ref,
                         wa_ref, wb_ref, wc_ref, cw_ref, wo_ref, x_ref, g_ref, xo_ref, ho_ref,
                         a_sc, b_sc, c_sc, mix_sc, *, tm, seq):
    d = x_ref.shape[1]
    for hh in range(FOX_HEADS):
        a_sc[:, hh * HEAD_DIM:(hh + 1) * HEAD_DIM] = a_ref[hh]
    for g in range(NSA_KV_GROUPS):
        for hg in range(NSA_HG):
            c0 = (g * NSA_HG + hg) * HEAD_DIM
            for qb in range(tm // Q_BLOCK):
                c_sc[qb * Q_BLOCK:(qb + 1) * Q_BLOCK, c0:c0 + HEAD_DIM] = c_ref[g, qb, hg * Q_BLOCK:(hg + 1) * Q_BLOCK, :]
    seq_start = (pl.program_id(0) * tm) % seq == 0
    u = cc_ref[...] * cx_ref[...]
    halo = jnp.where(seq_start, 0.0, cch_ref[...] * cxh_ref[...])
    b_sc[...] = (cb_ref[...] * _causal_conv3(u, halo, cw_ref[...])).astype(BF16)

    h = h_ref[...]
    for col in range(d // MIX_COLS):
        cols = slice(col * MIX_COLS, (col + 1) * MIX_COLS)
        mix = jnp.zeros((tm, MIX_COLS), F32)
        for branch, (src, w_ref) in enumerate(((a_sc, wa_ref), (b_sc, wb_ref), (c_sc, wc_ref))):
            gcols = slice(branch * d + col * MIX_COLS, branch * d + (col + 1) * MIX_COLS)
            gate = jnp.dot(h, wg_ref[:, gcols], preferred_element_type=F32) + bg_ref[:, gcols]
            mix = mix + _sigmoid(gate) * jnp.dot(src[...], w_ref[:, cols], preferred_element_type=F32)
        mix_sc[:, cols] = mix.astype(BF16)

    xn = x_ref[...] + jnp.dot(mix_sc[...], wo_ref[...], preferred_element_type=F32)
    xo_ref[...] = xn
    ho_ref[...] = _rms(xn, g_ref[...]).astype(ho_ref.dtype)


def mixer_output(h, w_gate, b_gate, zc, a_out, c_out, w_a, w_b, w_c, conv_w, w_o, x, g, seq):
    n, d = x.shape
    cw = w_b.shape[0]
    tm = min(MIX_ROWS, seq)
    tpb = seq // tm
    nb = tm // Q_BLOCK
    row = lambda width: pl.BlockSpec((tm, width), lambda i: (i, 0))
    ccol = lambda c: pl.BlockSpec((tm, cw), lambda i: (i, c))
    hcol = lambda c: pl.BlockSpec((SUBLANES, cw), lambda i: (jnp.maximum(i * (tm // SUBLANES) - 1, 0), c))
    once = lambda shape: pl.BlockSpec(shape, lambda i: (0, 0), pipeline_mode=pl.Buffered(1))
    return pl.pallas_call(
        functools.partial(_mixer_output_kernel, tm=tm, seq=seq),
        out_shape=(jax.ShapeDtypeStruct((n, d), F32), jax.ShapeDtypeStruct((n, d), BF16)),
        grid=(n // tm,),
        in_specs=[row(d), once(w_gate.shape), once((1, 3 * d)),
                  ccol(0), ccol(1), ccol(2), hcol(1), hcol(2),
                  pl.BlockSpec((None, FOX_HEADS, tm, HEAD_DIM), lambda i: (i // tpb, 0, i % tpb, 0)),
                  pl.BlockSpec((None, NSA_KV_GROUPS, nb, NSA_ROWS, HEAD_DIM), lambda i: (i // tpb, 0, i % tpb, 0, 0)),
                  once(w_a.shape), once(w_b.shape), once(w_c.shape), once(conv_w.shape), once(w_o.shape),
                  row(d), once((1, d))],
        out_specs=(row(d), row(d)),
        scratch_shapes=[pltpu.VMEM((tm, FOX_HEADS * HEAD_DIM), BF16), pltpu.VMEM((tm, cw), BF16),
                        pltpu.VMEM((tm, NSA_HEADS * HEAD_DIM), BF16), pltpu.VMEM((tm, d), BF16)],
        compiler_params=_cparams(1), name="mixer_output",
    )(h, w_gate, b_gate.reshape(1, -1), zc, zc, zc, zc, zc, a_out, c_out, w_a, w_b, w_c, conv_w, w_o,
      x, g.reshape(1, d))


def _ffn_up_kernel(h_ref, hh_ref, wu_ref, wv_ref, cw_ref, o_ref, *, tm, seq):
    seq_start = (pl.program_id(1) * tm) % seq == 0
    wu = wu_ref[...]
    u = jnp.dot(h_ref[...], wu, preferred_element_type=F32)
    v = jnp.dot(h_ref[...], wv_ref[...], preferred_element_type=F32)
    halo = jnp.where(seq_start, 0.0, jnp.dot(hh_ref[...], wu, preferred_element_type=F32))
    c = _causal_conv3(u, halo, cw_ref[...])
    o_ref[...] = (c * _sigmoid(c) * v).astype(o_ref.dtype)


def ffn_up(h, w_up, conv_w, seq, tm=1024):
    n, d = h.shape
    f = conv_w.shape[1]
    tm = min(tm, n)
    nj = 2 if (f // 2) % LANES == 0 else 1
    tn = f // nj
    return pl.pallas_call(
        functools.partial(_ffn_up_kernel, tm=tm, seq=seq),
        out_shape=jax.ShapeDtypeStruct((n, f), BF16),
        grid=(nj, n // tm),
        in_specs=[pl.BlockSpec((tm, d), lambda j, i: (i, 0)),
                  pl.BlockSpec((2 * SUBLANES, d), lambda j, i: (jnp.maximum(i * (tm // (2 * SUBLANES)) - 1, 0), 0)),
                  pl.BlockSpec((d, tn), lambda j, i: (0, j)),
                  pl.BlockSpec((d, tn), lambda j, i: (0, nj + j)),
                  pl.BlockSpec((conv_w.shape[0], tn), lambda j, i: (0, j))],
        out_specs=pl.BlockSpec((tm, tn), lambda j, i: (i, j)),
        compiler_params=_cparams(2), name="ffn_up",
    )(h, h, w_up, w_up, conv_w)


def _selection_map(n_sel, n_cmp_pad):
    sel_start = np.arange(n_sel)[:, None] * SEL_LEN
    cmp_start = np.arange(n_cmp_pad)[None, :] * CMP_STRIDE
    ov = np.minimum(sel_start + SEL_LEN, cmp_start + CMP_LEN) - np.maximum(sel_start, cmp_start)
    ov = np.clip(ov, 0, None) / CMP_STRIDE
    ov[:, n_cmp_pad - 1] = 0.0
    return ov.astype(np.float32)


def _bf16_exact(a):
    a = np.asarray(a, np.float32)
    assert np.array_equal(a, a.astype(BF16).astype(np.float32)), "constant is not exact in bf16"
    return a


def _after_head(cols, width=LANES):
    out = np.zeros(cols.shape[:-1] + (width,), np.float32)
    out[..., HEAD_DIM:HEAD_DIM + cols.shape[-1]] = cols
    return out


def _alibi_query_cols(nqb):
    slope = np.array([2.0 ** (-8.0 * (i + 1) / NSA_HEADS) for i in range(NSA_HEADS)], np.float32)
    slope = slope.reshape(NSA_KV_GROUPS, 1, NSA_HG, 1)
    qpos = (np.arange(nqb)[:, None] * Q_BLOCK + np.arange(Q_BLOCK)[None, :]).reshape(1, nqb, 1, Q_BLOCK)
    cols = np.stack(np.broadcast_arrays(slope * SEL_LEN, slope, -slope * SEL_LEN * (qpos // SEL_LEN),
                                        -slope * (qpos % SEL_LEN)), axis=-1)
    return _bf16_exact(cols.reshape(NSA_KV_GROUPS, nqb, NSA_ROWS, 4))


def _alibi_key_cols(pos):
    pos = np.asarray(pos)
    return _bf16_exact(np.stack([pos // SEL_LEN, pos % SEL_LEN, np.ones_like(pos), np.ones_like(pos)], axis=-1))


def _split_w_in(w_in_l, b_in_l, d):
    sizes = (d, d, d, 512, 512, 512, FOX_HEADS, 512, 512, 512, 512, 128, 128, 128, 128, 128, 128, NSA_HEADS * 3)
    offs = np.concatenate([[0], np.cumsum(sizes)])
    col = lambda a, k: a[..., int(offs[k]):int(offs[k + 1])]
    (GA, GB, GC, FQ, FK, FV, FF, CB, CC, CX, NQ, NKC, NVC, NKS, NVS, NKW, NVW, NG) = range(18)
    pad = LANES - (FOX_HEADS + NSA_HEADS * 3)

    prescale = {FQ: SCALE, NQ: SCALE}

    def group(idx, extra=0):
        w = jnp.concatenate([col(w_in_l, k) * prescale.get(k, 1.0) for k in idx], axis=-1)
        bias = jnp.concatenate([col(b_in_l, k) * prescale.get(k, 1.0) for k in idx], axis=-1)
        if extra:
            w = jnp.pad(w, ((0, 0), (0, extra)))
            bias = jnp.pad(bias, ((0, extra),))
        return w.astype(BF16), bias

    return (group((FQ, FK, FV)), group((NQ, NKS, NVS, NKW, NVW, NKC, NVC)),
            group((GA, GB, GC)), group((CB, CC, CX)), group((FF, NG), pad))


def kernel(x, norm_mix_g, w_in, b_in, cmp_pe_k, cmp_w1_k, cmp_w2_k, cmp_pe_v, cmp_w1_v, cmp_w2_v, sc_conv_w, w_br_a, w_br_b, w_br_c, w_o, norm_ffn_g, w_up, ffn_conv_w, w_down, norm_final_g):
    b, s, d = x.shape
    depth = w_in.shape[0]
    n = b * s
    nqb = s // Q_BLOCK
    n_cmp_pad = s // CMP_STRIDE
    const = lambda a: jnp.asarray(a, dtype=BF16)
    selmap = const(_selection_map(s // SEL_LEN, n_cmp_pad))
    one_col = const(_after_head(np.ones((1, 1), np.float32)))
    q_cols = const(_after_head(_alibi_query_cols(nqb)))
    cmp_cols = const(_after_head(_alibi_key_cols(np.arange(n_cmp_pad) * CMP_STRIDE + CMP_LEN - 1)))
    key_cols = _after_head(_alibi_key_cols(np.arange(s)))
    key_block = (np.arange(s)[:, None] // SEL_LEN == np.arange(LANES)[None, :]).astype(np.float32)
    sel_cols = const(np.concatenate([key_cols, key_block], axis=-1))
    key_cols = const(key_cols)

    xr = x.reshape(n, d)
    h = rmsnorm_rows(xr, norm_mix_g[0], BF16)
    for l in range(depth):
        ((w_fox, b_fox), (w_nsa, b_nsa), (w_gate, b_gate), (w_conv, b_conv),
         (w_sm, b_sm)) = _split_w_in(w_in[l], b_in[l], d)
        zc = matmul_bias(h, w_conv, b_conv, F32, tn=w_conv.shape[1])
        zs3 = matmul_bias(h, w_sm, b_sm, F32, tn=LANES).reshape(b, s, -1)

        bias_cols = fox_bias_columns(zs3[..., 0:FOX_HEADS].transpose(0, 2, 1))
        a_out = fox_attention(fox_qkv_projection(h, w_fox, b_fox, bias_cols, b, s))

        q_aug, ks_aug, vs_aug, kw_aug, vw_aug, cmp_in = nsa_projection(
            h, w_nsa, b_nsa, q_cols, sel_cols, key_cols, one_col, b, s)
        kc_aug, vc_aug = compress_blocks(
            cmp_in.reshape(2, b, NSA_KV_GROUPS, n_cmp_pad, CMP_STRIDE * HEAD_DIM),
            cmp_pe_k[l].reshape(1, -1), cmp_pe_v[l].reshape(1, -1),
            cmp_w1_k[l].astype(BF16), cmp_w2_k[l].astype(BF16),
            cmp_w1_v[l].astype(BF16), cmp_w2_v[l].astype(BF16), cmp_cols, one_col)
        ng = zs3[..., FOX_HEADS:FOX_HEADS + NSA_HEADS * 3].reshape(b, nqb, Q_BLOCK, NSA_KV_GROUPS, NSA_HG, 3)
        ng = ng.transpose(0, 3, 1, 4, 2, 5).reshape(b, NSA_KV_GROUPS, nqb, NSA_ROWS, 3)
        c_out = nsa_attention(q_aug, kc_aug, vc_aug, ks_aug, vs_aug, kw_aug, vw_aug, ng, selmap)

        xr, h2 = mixer_output(h, w_gate, b_gate, zc, a_out, c_out, w_br_a[l].astype(BF16),
                              w_br_b[l].astype(BF16), w_br_c[l].astype(BF16), sc_conv_w[l],
                              w_o[l].astype(BF16), xr, norm_ffn_g[l], s)

        act = ffn_up(h2, w_up[l].astype(BF16), ffn_conv_w[l], s)
        if l + 1 < depth:
            xr, h = residual_matmul_norm(act, w_down[l].astype(BF16), xr, norm_mix_g[l + 1], BF16)
        else:
            out = residual_matmul_norm(act, w_down[l].astype(BF16), xr, norm_final_g, F32,
                                       keep_residual=False)
    return out.reshape(b, s, d)
```
